```python
import jax, jax.numpy as jnp
from jax import lax
import numpy as np

D_MODEL = 1024
BATCH = 16
SEQ = 2048
DEPTH = 2
DEC_BATCH = 32
DEC_SEQ = 64
PAST_LEN = 4096

CHUNK = 64
N_PREV_CHUNKS = 8
N_BAND = N_PREV_CHUNKS + 1
BAND_ROWS = N_PREV_CHUNKS * CHUNK
D_MIX = D_MODEL
HEADS_A = 8
HEAD_DIM_A = 64
WIDTH_A = HEADS_A * HEAD_DIM_A
HEADS_R = 4
HEAD_DIM_R = 128
WIDTH_R = HEADS_R * HEAD_DIM_R
N_PROJ = 3 * WIDTH_A + 4 * WIDTH_R
REL_CLIP = 2 * CHUNK
D_FF = -(-8 * D_MODEL // (3 * 256)) * 256
ROPE_BASE = 10000.0
RMS_EPS = 1e-6
GN_EPS = 1e-5
NEG_INF = -1e30

kernel_name = "hybrid_band_attn_retention_stream_step"


def rms_norm(x, g):
    xf = x.astype(jnp.float32)
    y = xf * lax.rsqrt(jnp.mean(xf * xf, axis=-1, keepdims=True) + RMS_EPS)
    return (y * g.astype(jnp.float32)).astype(x.dtype)


def rope(x, pos):
    half = x.shape[-1] // 2
    inv = ROPE_BASE ** (-jnp.arange(half, dtype=jnp.float32) / half)
    ang = pos[..., None] * inv
    cos, sin = jnp.cos(ang), jnp.sin(ang)
    xf = x.astype(jnp.float32)
    x1, x2 = xf[..., :half], xf[..., half:]
    return jnp.concatenate([x1 * cos - x2 * sin, x1 * sin + x2 * cos], axis=-1).astype(x.dtype)


def rebase_state(S, shift):
    St = jnp.swapaxes(S, -1, -2)
    return jnp.swapaxes(rope(St, jnp.asarray(shift, dtype=jnp.float32)), -1, -2)


def rel_bias(rel_table, q_rel, k_rel):
    dist = q_rel[:, None] - k_rel[None, :]
    idx = jnp.clip(dist, -REL_CLIP, REL_CLIP) + REL_CLIP
    return rel_table[:, idx].astype(jnp.float32)


def retention_log_gamma():
    return jnp.log(1.0 - 2.0 ** (-5.0 - jnp.arange(HEADS_R, dtype=jnp.float32)))


def retention_block(q, k, v, S, log_gamma):
    L = q.shape[1]
    idx = jnp.arange(L, dtype=jnp.float32)
    diff = idx[:, None] - idx[None, :]
    decay = jnp.where(diff[None] >= 0,
                      jnp.exp(jnp.maximum(diff, 0.0)[None] * log_gamma[:, None, None]), 0.0)
    scores = jnp.einsum("blhd,bmhd->bhlm", q, k) * decay[None]
    intra = jnp.einsum("bhlm,bmhe->blhe", scores, v)
    q_decay = jnp.exp((idx + 1.0)[:, None] * log_gamma[None, :])
    cross = jnp.einsum("blhd,bhde->blhe", q, S) * q_decay[None, :, :, None]
    k_decay = jnp.exp((L - 1.0 - idx)[:, None] * log_gamma[None, :])
    S_new = (jnp.exp(L * log_gamma)[None, :, None, None] * S
             + jnp.einsum("blhd,blhe->bhde", k * k_decay[None, :, :, None], v))
    return intra + cross, S_new


def project(x, g_mix, w_in, q_norm, k_norm):
    B, L, _ = x.shape
    z = rms_norm(x, g_mix) @ w_in
    cuts = [WIDTH_A, 2 * WIDTH_A, 3 * WIDTH_A, 3 * WIDTH_A + WIDTH_R,
            3 * WIDTH_A + 2 * WIDTH_R, 3 * WIDTH_A + 3 * WIDTH_R]
    qa, ka, va, qr, kr, vr, gr = jnp.split(z, cuts, axis=-1)
    qa = rms_norm(qa.reshape(B, L, HEADS_A, HEAD_DIM_A), q_norm)
    ka = rms_norm(ka.reshape(B, L, HEADS_A, HEAD_DIM_A), k_norm)
    va = va.reshape(B, L, HEADS_A, HEAD_DIM_A)
    qr = qr.reshape(B, L, HEADS_R, HEAD_DIM_R)
    kr = kr.reshape(B, L, HEADS_R, HEAD_DIM_R)
    vr = vr.reshape(B, L, HEADS_R, HEAD_DIM_R)
    return qa, ka, va, qr, kr, vr, gr


def band_attention_prompt(q, k, v, rel_table):
    B, S = q.shape[0], q.shape[1]
    NC = S // CHUNK
    qc = q.reshape(B, NC, CHUNK, HEADS_A, HEAD_DIM_A)
    pad = ((0, 0), (N_PREV_CHUNKS, 0), (0, 0), (0, 0), (0, 0))
    band = jnp.arange(NC)[:, None] + jnp.arange(N_BAND)[None, :]
    kb = jnp.pad(k.reshape(B, NC, CHUNK, HEADS_A, HEAD_DIM_A), pad)[:, band]
    vb = jnp.pad(v.reshape(B, NC, CHUNK, HEADS_A, HEAD_DIM_A), pad)[:, band]
    kb = kb.reshape(B, NC, N_BAND * CHUNK, HEADS_A, HEAD_DIM_A)
    vb = vb.reshape(B, NC, N_BAND * CHUNK, HEADS_A, HEAD_DIM_A)
    k_rel = jnp.arange(N_BAND * CHUNK) - BAND_ROWS
    bias = rel_bias(rel_table, jnp.arange(CHUNK), k_rel)
    valid = jnp.repeat(band >= N_PREV_CHUNKS, CHUNK, axis=1)
    logits = jnp.einsum("bnqhd,bnkhd->bnhqk", qc, kb).astype(jnp.float32) * (HEAD_DIM_A ** -0.5) + bias
    logits = jnp.where(valid[None, :, None, None, :], logits, NEG_INF)
    p = jax.nn.softmax(logits, axis=-1).astype(v.dtype)
    o = jnp.einsum("bnhqk,bnkhd->bnqhd", p, vb)
    return o.reshape(B, S, WIDTH_A)


def band_attention_sample(q, k, v, cache_k, cache_v, rel_table):
    B, L = q.shape[0], q.shape[1]
    R = cache_k.shape[1]
    kk = jnp.concatenate([cache_k.astype(k.dtype), k], axis=1)
    vv = jnp.concatenate([cache_v.astype(v.dtype), v], axis=1)
    k_rel = jnp.concatenate([jnp.arange(R) - R, jnp.arange(L)])
    bias = rel_bias(rel_table, jnp.arange(L), k_rel)
    logits = jnp.einsum("bqhd,bkhd->bhqk", q, kk).astype(jnp.float32) * (HEAD_DIM_A ** -0.5) + bias
    p = jax.nn.softmax(logits, axis=-1).astype(v.dtype)
    return jnp.einsum("bhqk,bkhd->bqhd", p, vv).reshape(B, L, WIDTH_A)


def retention_prompt(q, k, v):
    B, S = q.shape[0], q.shape[1]
    NC = S // CHUNK
    pos = jnp.arange(S, dtype=jnp.float32)[:, None]
    q = rope(q, pos).astype(jnp.float32)
    k = rope(k, pos).astype(jnp.float32) * (HEAD_DIM_R ** -0.5)
    v = v.astype(jnp.float32)
    to_chunks = lambda t: t.reshape(B, NC, CHUNK, HEADS_R, HEAD_DIM_R).transpose(1, 0, 2, 3, 4)
    log_gamma = retention_log_gamma()

    def step(S_c, inp):
        qc, kc, vc = inp
        o_c, S_n = retention_block(qc, kc, vc, S_c, log_gamma)
        return S_n, o_c

    S0 = jnp.zeros((B, HEADS_R, HEAD_DIM_R, HEAD_DIM_R), jnp.float32)
    S_fin, o = lax.scan(step, S0, (to_chunks(q), to_chunks(k), to_chunks(v)))
    o = o.transpose(1, 0, 2, 3, 4).reshape(B, S, HEADS_R, HEAD_DIM_R)
    return o, rebase_state(S_fin, -S)


def retention_sample(q, k, v, state):
    L = q.shape[1]
    pos = jnp.arange(L, dtype=jnp.float32)[:, None]
    q = rope(q, pos).astype(jnp.float32)
    k = rope(k, pos).astype(jnp.float32) * (HEAD_DIM_R ** -0.5)
    o, S_new = retention_block(q, k, v.astype(jnp.float32), state.astype(jnp.float32), retention_log_gamma())
    return o, rebase_state(S_new, -L)


def merge_and_ffn(x, o_a, o_r, g_r, ret_norm, w_out, g_ffn, w_gate, w_up, w_down):
    B, L, _ = x.shape
    mu = jnp.mean(o_r, axis=-1, keepdims=True)
    var = jnp.mean(jnp.square(o_r - mu), axis=-1, keepdims=True)
    o_r = ((o_r - mu) * lax.rsqrt(var + GN_EPS)).reshape(B, L, WIDTH_R) * ret_norm.astype(jnp.float32)
    o_r = jax.nn.silu(g_r) * o_r.astype(x.dtype)
    x = x + jnp.concatenate([o_a, o_r], axis=-1) @ w_out
    h = rms_norm(x, g_ffn)
    return x + (jax.nn.silu(h @ w_gate) * (h @ w_up)) @ w_down


def setup_inputs(seed: int = 0) -> dict:
    key = jax.random.key(seed)
    ks = jax.random.split(key, 16)
    a_rows = min(BAND_ROWS, PAST_LEN)
    nrm = lambda k, shape, s: jax.random.normal(k, shape, jnp.float32) * s
    return {
        "x_prompt": nrm(ks[0], (BATCH, SEQ, D_MODEL), 1.0),
        "x_sample": nrm(ks[1], (DEC_BATCH, DEC_SEQ, D_MODEL), 1.0),
        "cache_a_k": nrm(ks[2], (DEPTH, DEC_BATCH, a_rows, HEADS_A, HEAD_DIM_A), 1.0),
        "cache_a_v": nrm(ks[3], (DEPTH, DEC_BATCH, a_rows, HEADS_A, HEAD_DIM_A), 1.0),
        "state_ret": nrm(ks[4], (DEPTH, DEC_BATCH, HEADS_R, HEAD_DIM_R, HEAD_DIM_R), 0.5),
        "norm_mix": 1.0 + nrm(ks[5], (DEPTH, D_MODEL), 0.02),
        "w_in": nrm(ks[6], (DEPTH, D_MODEL, N_PROJ), D_MODEL ** -0.5),
        "q_norm": 1.0 + nrm(ks[7], (DEPTH, HEAD_DIM_A), 0.02),
        "k_norm": 1.0 + nrm(ks[8], (DEPTH, HEAD_DIM_A), 0.02),
        "rel_table": nrm(ks[9], (DEPTH, HEADS_A, 2 * REL_CLIP + 1), 0.1),
        "ret_norm": 1.0 + nrm(ks[10], (DEPTH, WIDTH_R), 0.02),
        "w_out": nrm(ks[11], (DEPTH, D_MIX, D_MODEL), D_MIX ** -0.5),
        "norm_ffn": 1.0 + nrm(ks[12], (DEPTH, D_MODEL), 0.02),
        "w_gate": nrm(ks[13], (DEPTH, D_MODEL, D_FF), D_MODEL ** -0.5),
        "w_up": nrm(ks[14], (DEPTH, D_MODEL, D_FF), D_MODEL ** -0.5),
        "w_down": nrm(ks[15], (DEPTH, D_FF, D_MODEL), D_FF ** -0.5),
    }


def reference(x_prompt, x_sample, cache_a_k, cache_a_v, state_ret, norm_mix, w_in, q_norm, k_norm,
              rel_table, ret_norm, w_out, norm_ffn, w_gate, w_up, w_down):
    xp, xs = x_prompt, x_sample
    rows_p = min(BAND_ROWS, x_prompt.shape[1])
    kp_rows, vp_rows, sp_list, ks_rows, vs_rows, ss_list = [], [], [], [], [], []
    for l in range(DEPTH):
        qa, ka, va, qr, kr, vr, gr = project(xp, norm_mix[l], w_in[l], q_norm[l], k_norm[l])
        o_a = band_attention_prompt(qa, ka, va, rel_table[l])
        o_r, S_p = retention_prompt(qr, kr, vr)
        xp = merge_and_ffn(xp, o_a, o_r, gr, ret_norm[l], w_out[l], norm_ffn[l], w_gate[l], w_up[l], w_down[l])
        kp_rows.append(ka[:, -rows_p:])
        vp_rows.append(va[:, -rows_p:])
        sp_list.append(S_p)
        qa, ka, va, qr, kr, vr, gr = project(xs, norm_mix[l], w_in[l], q_norm[l], k_norm[l])
        o_a = band_attention_sample(qa, ka, va, cache_a_k[l], cache_a_v[l], rel_table[l])
        o_r, S_s = retention_sample(qr, kr, vr, state_ret[l])
        xs = merge_and_ffn(xs, o_a, o_r, gr, ret_norm[l], w_out[l], norm_ffn[l], w_gate[l], w_up[l], w_down[l])
        ks_rows.append(ka)
        vs_rows.append(va)
        ss_list.append(S_s)
    return (xp, xs,
            jnp.stack(kp_rows), jnp.stack(vp_rows), jnp.stack(sp_list),
            jnp.stack(ks_rows), jnp.stack(vs_rows), jnp.stack(ss_list))
```

```python
import functools

import jax
import jax.numpy as jnp
import numpy as np
from jax import lax
from jax.experimental import pallas as pl
from jax.experimental.pallas import tpu as pltpu

F32 = jnp.float32
BF16 = jnp.bfloat16

D_MODEL = 1024
CHUNK = 64
N_PREV_CHUNKS = 8
BAND_ROWS = N_PREV_CHUNKS * CHUNK
HEADS_A = 8
HEAD_DIM_A = 64
WIDTH_A = HEADS_A * HEAD_DIM_A
HEADS_R = 4
HEAD_DIM_R = 128
WIDTH_R = HEADS_R * HEAD_DIM_R
N_SEG = 7
SEG = 512
REL_CLIP = 2 * CHUNK
ROPE_BASE = 10000.0
RMS_EPS = 1e-6
GN_EPS = 1e-5
NEG_INF = -1e30

VMEM_LIMIT_BYTES = 56 * 1024 * 1024
LANES = 128
MXU_DIM = 256


def _resident(shape):
    nd = len(shape)
    return pl.BlockSpec(shape, lambda *_: (0,) * nd, pipeline_mode=pl.Buffered(1))


def _params(n_axes):
    return pltpu.CompilerParams(dimension_semantics=("arbitrary",) * n_axes,
                                vmem_limit_bytes=VMEM_LIMIT_BYTES)


def _proj_kernel(x_ref, g_ref, w_ref, qn_ref, kn_ref, cos_ref, sin_ref, hm_ref,
                 qa_ref, ka_ref, va_ref, qr_ref, kr_ref, vr_ref, gr_ref, kt_ref, vt_ref,
                 *, tail_period, tail_first):
    x = x_ref[...]
    ms = jnp.mean(x * x, axis=-1, keepdims=True)
    h = ((x * lax.rsqrt(ms + RMS_EPS)) * g_ref[...]).astype(BF16)

    def seg(j):
        return jnp.dot(h, w_ref[:, j * SEG:(j + 1) * SEG], preferred_element_type=F32)

    def head_rms(z, gain):
        z2 = (z * z).astype(BF16)
        parts = [jnp.dot(z2[:, c * MXU_DIM:(c + 1) * MXU_DIM], hm_ref[...], preferred_element_type=F32)
                 for c in range(SEG // MXU_DIM)]
        msq = jnp.concatenate(parts, axis=-1)
        return (z * lax.rsqrt(msq + RMS_EPS)) * gain

    def rotary(z):
        outs = []
        for hd in range(HEADS_R):
            zh = z[:, hd * HEAD_DIM_R:(hd + 1) * HEAD_DIM_R]
            outs.append(zh * cos_ref[...] + pltpu.roll(zh, HEAD_DIM_R // 2, axis=1) * sin_ref[...])
        return jnp.concatenate(outs, axis=-1)

    qa = head_rms(seg(0), qn_ref[...]) * (HEAD_DIM_A ** -0.5)
    qa_ref[...] = qa.astype(BF16)
    ka = head_rms(seg(1), kn_ref[...])
    ka_ref[...] = ka.astype(BF16)
    va = seg(2)
    va_ref[...] = va.astype(BF16)

    if tail_period == 1:
        kt_ref[...] = ka
        vt_ref[...] = va
    else:
        @pl.when(pl.program_id(0) % tail_period >= tail_first)
        def _():
            kt_ref[...] = ka
            vt_ref[...] = va

    qr_ref[...] = rotary(seg(3)).astype(BF16)
    kr_ref[...] = rotary(seg(4)) * (HEAD_DIM_R ** -0.5)
    vr_ref[...] = seg(5).astype(BF16)
    gr_ref[...] = seg(6)


def _proj(x2d, g_mix, w_in, q_norm, k_norm, seq, tail_rows, tm=512):
    t_rows = x2d.shape[0]
    assert t_rows % tm == 0
    n_tiles = t_rows // tm
    if seq >= tm:
        assert seq % tm == 0 and tail_rows % tm == 0
        tail_period = seq // tm
        tail_first = (seq - tail_rows) // tm
        tail_blocks = tail_rows // tm
        table_rows = seq
    else:
        assert tm % seq == 0 and tail_rows == seq
        tail_period, tail_first, tail_blocks = 1, 0, 1
        table_rows = tm
    n_tab = table_rows // tm

    half = HEAD_DIM_R // 2
    inv = ROPE_BASE ** (-jnp.arange(half, dtype=F32) / half)
    pos = (jnp.arange(table_rows) % seq).astype(F32)
    ang = pos[:, None] * inv
    cos, sin = jnp.cos(ang), jnp.sin(ang)
    cos_t = jnp.concatenate([cos, cos], axis=-1)
    sin_t = jnp.concatenate([-sin, sin], axis=-1)

    lane = np.arange(MXU_DIM)
    head_mean = jnp.asarray((lane[:, None] // HEAD_DIM_A == lane[None, :] // HEAD_DIM_A)
                            .astype(np.float32) / HEAD_DIM_A, dtype=BF16)

    row = lambda i: (i, 0)
    tail_map = lambda i: ((i // tail_period) * tail_blocks
                          + jnp.maximum(i % tail_period - tail_first, 0), 0)
    out_bf = jax.ShapeDtypeStruct((t_rows, SEG), BF16)
    out_f32 = jax.ShapeDtypeStruct((t_rows, SEG), F32)
    tail_shape = jax.ShapeDtypeStruct((t_rows // seq * tail_rows, SEG), F32)
    blk = pl.BlockSpec((tm, SEG), row)
    return pl.pallas_call(
        functools.partial(_proj_kernel, tail_period=tail_period, tail_first=tail_first),
        grid=(n_tiles,),
        in_specs=[
            pl.BlockSpec((tm, D_MODEL), row),
            _resident((1, D_MODEL)),
            _resident((D_MODEL, N_SEG * SEG)),
            _resident((1, SEG)),
            _resident((1, SEG)),
            pl.BlockSpec((tm, HEAD_DIM_R), lambda i: (i % n_tab, 0)),
            pl.BlockSpec((tm, HEAD_DIM_R), lambda i: (i % n_tab, 0)),
            _resident((MXU_DIM, MXU_DIM)),
        ],
        out_specs=[blk, blk, blk, blk, blk, blk, blk,
                   pl.BlockSpec((tm, SEG), tail_map), pl.BlockSpec((tm, SEG), tail_map)],
        out_shape=[out_bf, out_bf, out_bf, out_bf, out_f32, out_bf, out_f32, tail_shape, tail_shape],
        compiler_params=_params(1),
        name="proj",
    )(x2d, g_mix.reshape(1, D_MODEL), w_in.astype(BF16),
      jnp.tile(q_norm, HEADS_A).reshape(1, SEG), jnp.tile(k_norm, HEADS_A).reshape(1, SEG),
      cos_t, sin_t, head_mean)


def _attn_kernel(*refs, tq, nk, past, seq, has_cache):
    if has_cache:
        q_ref, k_ref, v_ref, bias_ref, ck_ref, cv_ref, o_ref, kbuf, vbuf = refs
    else:
        q_ref, k_ref, v_ref, bias_ref, o_ref, kbuf, vbuf = refs
    t = pl.program_id(1)

    @pl.when(t == 0)
    def _():
        if has_cache:
            kbuf[0:past, :] = ck_ref[0].astype(BF16)
            vbuf[0:past, :] = cv_ref[0].astype(BF16)
        else:
            kbuf[0:past, :] = jnp.zeros((past, WIDTH_A), BF16)
            vbuf[0:past, :] = jnp.zeros((past, WIDTH_A), BF16)
        kbuf[past:past + seq, :] = k_ref[0]
        vbuf[past:past + seq, :] = v_ref[0]

    start = pl.multiple_of(t * tq, tq)
    q = q_ref[0]
    kb = kbuf[pl.ds(start, nk), :]
    vb = vbuf[pl.ds(start, nk), :]
    lane = lax.broadcasted_iota(jnp.int32, (tq, LANES), 1)
    if not has_cache:
        col = lax.broadcasted_iota(jnp.int32, (tq, nk), 1)
        real = col + start >= past
    outs = []
    for pair in range(HEADS_A // 2):
        sl = slice(pair * LANES, (pair + 1) * LANES)
        qp, kp, vp = q[:, sl], kb[:, sl], vb[:, sl]
        res = []
        for e in range(2):
            in_head = (lane >= e * HEAD_DIM_A) & (lane < (e + 1) * HEAD_DIM_A)
            qz = jnp.where(in_head, qp, jnp.zeros_like(qp))
            s = lax.dot_general(qz, kp, (((1,), (1,)), ((), ())), preferred_element_type=F32)
            s = s + bias_ref[2 * pair + e]
            if not has_cache:
                s = jnp.where(real, s, NEG_INF)
            m = jnp.max(s, axis=-1, keepdims=True)
            p = jnp.exp(s - m)
            denom = jnp.sum(p, axis=-1, keepdims=True)
            res.append(jnp.dot(p.astype(BF16), vp, preferred_element_type=F32) / denom)
        outs.append(jnp.where(lane < HEAD_DIM_A, res[0], res[1]))
    o_ref[0] = jnp.concatenate(outs, axis=-1).astype(BF16)


def _band_bias(rel_table, chunks_per_tile):
    q_rel = jnp.arange(CHUNK)
    k_rel = jnp.arange((N_PREV_CHUNKS + 1) * CHUNK) - BAND_ROWS
    idx = jnp.clip(q_rel[:, None] - k_rel[None, :], -REL_CLIP, REL_CLIP) + REL_CLIP
    base = rel_table[:, idx].astype(F32)
    rows = [jnp.pad(base, ((0, 0), (0, 0), (i * CHUNK, (chunks_per_tile - 1 - i) * CHUNK)),
                    constant_values=NEG_INF) for i in range(chunks_per_tile)]
    return jnp.concatenate(rows, axis=1)


def _attention(qa, ka, va, rel_table, cache_k=None, cache_v=None, chunks_per_tile=2):
    b, seq, _ = qa.shape
    has_cache = cache_k is not None
    cq = min(chunks_per_tile, seq // CHUNK)
    tq = cq * CHUNK
    nk = (N_PREV_CHUNKS + cq) * CHUNK
    bias = _band_bias(rel_table, cq)
    tile = pl.BlockSpec((1, tq, WIDTH_A), lambda i, t: (i, t, 0))
    whole = pl.BlockSpec((1, seq, WIDTH_A), lambda i, t: (i, 0, 0))
    in_specs = [tile, whole, whole, _resident(bias.shape)]
    args = [qa, ka, va, bias]
    if has_cache:
        cache = pl.BlockSpec((1, BAND_ROWS, WIDTH_A), lambda i, t: (i, 0, 0))
        in_specs += [cache, cache]
        args += [cache_k, cache_v]
    return pl.pallas_call(
        functools.partial(_attn_kernel, tq=tq, nk=nk, past=BAND_ROWS, seq=seq, has_cache=has_cache),
        grid=(b, seq // tq),
        in_specs=in_specs,
        out_specs=tile,
        out_shape=jax.ShapeDtypeStruct((b, seq, WIDTH_A), BF16),
        scratch_shapes=[pltpu.VMEM((BAND_ROWS + seq, WIDTH_A), BF16),
                        pltpu.VMEM((BAND_ROWS + seq, WIDTH_A), BF16)],
        compiler_params=_params(2),
        name="band_attention",
    )(*args)


def _retention_kernel(*refs, n_chunks, has_state):
    if has_state:
        (q_ref, k_ref, v_ref, g_ref, dec_ref, qd_ref, kd_ref, sd_ref, rn_ref, rc_ref, rs_ref,
         s0_ref, o_ref, sout_ref, state) = refs
    else:
        (q_ref, k_ref, v_ref, g_ref, dec_ref, qd_ref, kd_ref, sd_ref, rn_ref, rc_ref, rs_ref,
         o_ref, sout_ref, state) = refs
    c = pl.program_id(1)

    @pl.when(c == 0)
    def _():
        if has_state:
            state[...] = s0_ref[0]
        else:
            state[...] = jnp.zeros(state.shape, F32)

    q = q_ref[0]
    k = k_ref[0]
    v = v_ref[0]
    outs = []
    for hd in range(HEADS_R):
        sl = slice(hd * HEAD_DIM_R, (hd + 1) * HEAD_DIM_R)
        qh, kh, vh = q[:, sl], k[:, sl], v[:, sl]
        s_prev = state[hd]
        scores = lax.dot_general(qh, kh.astype(BF16), (((1,), (1,)), ((), ())),
                                 preferred_element_type=F32) * dec_ref[hd]
        intra = jnp.dot(scores.astype(BF16), vh, preferred_element_type=F32)
        cross = jnp.dot(qh, s_prev.astype(BF16), preferred_element_type=F32) * qd_ref[:, sl]
        k_dec = (kh * kd_ref[:, sl]).astype(BF16)
        state[hd] = sd_ref[hd] * s_prev + lax.dot_general(
            k_dec, vh, (((0,), (0,)), ((), ())), preferred_element_type=F32)
        o = intra + cross
        mu = jnp.mean(o, axis=-1, keepdims=True)
        d = o - mu
        var = jnp.mean(d * d, axis=-1, keepdims=True)
        outs.append(d * lax.rsqrt(var + GN_EPS))
    o_norm = jnp.concatenate(outs, axis=-1) * rn_ref[...]
    g = g_ref[0]
    o_ref[0] = ((g * jax.nn.sigmoid(g)) * o_norm).astype(BF16)

    @pl.when(c == n_chunks - 1)
    def _():
        half = HEAD_DIM_R // 2
        for hd in range(HEADS_R):
            s1 = state[hd, 0:half, :]
            s2 = state[hd, half:HEAD_DIM_R, :]
            sout_ref[0, hd, 0:half, :] = s1 * rc_ref[...] - s2 * rs_ref[...]
            sout_ref[0, hd, half:HEAD_DIM_R, :] = s1 * rs_ref[...] + s2 * rc_ref[...]


def _retention(qr, kr, vr, gr, ret_norm, state0=None, block=256):
    b, seq, _ = qr.shape
    blk = min(block, seq)
    assert seq % blk == 0
    n_chunks = seq // blk
    has_state = state0 is not None

    log_gamma = jnp.log(1.0 - 2.0 ** (-5.0 - jnp.arange(HEADS_R, dtype=F32)))
    idx = jnp.arange(blk, dtype=F32)
    diff = idx[:, None] - idx[None, :]
    decay = jnp.where(diff[None] >= 0,
                      jnp.exp(jnp.maximum(diff, 0.0)[None] * log_gamma[:, None, None]), 0.0)
    lanes = lambda t: jnp.repeat(t, HEAD_DIM_R, axis=-1)
    q_decay = lanes(jnp.exp((idx + 1.0)[:, None] * log_gamma[None, :]))
    k_decay = lanes(jnp.exp((blk - 1.0 - idx)[:, None] * log_gamma[None, :]))
    s_decay = jnp.broadcast_to(jnp.exp(blk * log_gamma)[:, None, None], (HEADS_R, 1, HEAD_DIM_R))

    half = HEAD_DIM_R // 2
    inv = ROPE_BASE ** (-jnp.arange(half, dtype=F32) / half)
    ang = jnp.asarray(-seq, dtype=F32) * inv
    reb_cos = jnp.broadcast_to(jnp.cos(ang)[:, None], (half, HEAD_DIM_R))
    reb_sin = jnp.broadcast_to(jnp.sin(ang)[:, None], (half, HEAD_DIM_R))

    tile = pl.BlockSpec((1, blk, WIDTH_R), lambda i, c: (i, c, 0))
    st = pl.BlockSpec((1, HEADS_R, HEAD_DIM_R, HEAD_DIM_R), lambda i, c: (i, 0, 0, 0))
    in_specs = [tile, tile, tile, tile,
                _resident(decay.shape), _resident(q_decay.shape), _resident(k_decay.shape),
                _resident(s_decay.shape), _resident((1, WIDTH_R)),
                _resident(reb_cos.shape), _resident(reb_sin.shape)]
    args = [qr, kr, vr, gr, decay, q_decay, k_decay, s_decay, ret_norm.reshape(1, WIDTH_R),
            reb_cos, reb_sin]
    if has_state:
        in_specs.append(st)
        args.append(state0)
    return pl.pallas_call(
        functools.partial(_retention_kernel, n_chunks=n_chunks, has_state=has_state),
        grid=(b, n_chunks),
        in_specs=in_specs,
        out_specs=[tile, st],
        out_shape=[jax.ShapeDtypeStruct((b, seq, WIDTH_R), BF16),
                   jax.ShapeDtypeStruct((b, HEADS_R, HEAD_DIM_R, HEAD_DIM_R), F32)],
        scratch_shapes=[pltpu.VMEM((HEADS_R, HEAD_DIM_R, HEAD_DIM_R), F32)],
        compiler_params=_params(2),
        name="retention",
    )(*args)


def _merge_ffn_kernel(x_ref, oa_ref, or_ref, wo_ref, g_ref, wg_ref, wu_ref, wd_ref, y_ref, *, ff_chunk):
    mix = jnp.concatenate([oa_ref[...], or_ref[...]], axis=-1)
    x = x_ref[...] + jnp.dot(mix, wo_ref[...], preferred_element_type=F32)
    ms = jnp.mean(x * x, axis=-1, keepdims=True)
    h = ((x * lax.rsqrt(ms + RMS_EPS)) * g_ref[...]).astype(BF16)
    y = x
    d_ff = wg_ref.shape[1]
    for c in range(d_ff // ff_chunk):
        sl = slice(c * ff_chunk, (c + 1) * ff_chunk)
        gate = jnp.dot(h, wg_ref[:, sl], preferred_element_type=F32)
        up = jnp.dot(h, wu_ref[:, sl], preferred_element_type=F32)
        act = ((gate * jax.nn.sigmoid(gate)) * up).astype(BF16)
        y = y + jnp.dot(act, wd_ref[sl, :], preferred_element_type=F32)
    y_ref[...] = y


def _merge_ffn(x2d, o_a, o_r, w_out, g_ffn, w_gate, w_up, w_down, tm=512, ff_chunk=1408):
    t_rows = x2d.shape[0]
    d_ff = w_gate.shape[1]
    assert t_rows % tm == 0 and d_ff % ff_chunk == 0
    row = lambda i: (i, 0)
    return pl.pallas_call(
        functools.partial(_merge_ffn_kernel, ff_chunk=ff_chunk),
        grid=(t_rows // tm,),
        in_specs=[
            pl.BlockSpec((tm, D_MODEL), row),
            pl.BlockSpec((tm, WIDTH_A), row),
            pl.BlockSpec((tm, WIDTH_R), row),
            _resident((D_MODEL, D_MODEL)),
            _resident((1, D_MODEL)),
            _resident((D_MODEL, d_ff)),
            _resident((D_MODEL, d_ff)),
            _resident((d_ff, D_MODEL)),
        ],
        out_specs=pl.BlockSpec((tm, D_MODEL), row),
        out_shape=jax.ShapeDtypeStruct((t_rows, D_MODEL), F32),
        compiler_params=_params(1),
        name="merge_ffn",
    )(x2d, o_a, o_r, w_out.astype(BF16), g_ffn.reshape(1, D_MODEL),
      w_gate.astype(BF16), w_up.astype(BF16), w_down.astype(BF16))


def _layer(x, lw, tail_rows, cache_k=None, cache_v=None, state0=None):
    b, seq, _ = x.shape
    x2d = x.reshape(b * seq, D_MODEL)
    qa, ka, va, qr, kr, vr, gr, k_tail, v_tail = _proj(
        x2d, lw["norm_mix"], lw["w_in"], lw["q_norm"], lw["k_norm"], seq, tail_rows)
    r3 = lambda t: t.reshape(b, seq, SEG)
    if cache_k is not None:
        cache_k = cache_k.reshape(b, BAND_ROWS, WIDTH_A)
        cache_v = cache_v.reshape(b, BAND_ROWS, WIDTH_A)
    o_a = _attention(r3(qa), r3(ka), r3(va), lw["rel_table"], cache_k, cache_v)
    o_r, s_new = _retention(r3(qr), r3(kr), r3(vr), r3(gr), lw["ret_norm"], state0)
    y = _merge_ffn(x2d, o_a.reshape(b * seq, WIDTH_A), o_r.reshape(b * seq, WIDTH_R),
                   lw["w_out"], lw["norm_ffn"], lw["w_gate"], lw["w_up"], lw["w_down"])
    tails = lambda t: t.reshape(b, tail_rows, HEADS_A, HEAD_DIM_A)
    return y.reshape(b, seq, D_MODEL), tails(k_tail), tails(v_tail), s_new


def kernel(x_prompt, x_sample, cache_a_k, cache_a_v, state_ret, norm_mix, w_in, q_norm, k_norm,
           rel_table, ret_norm, w_out, norm_ffn, w_gate, w_up, w_down):
    depth = w_in.shape[0]
    assert cache_a_k.shape[2] == BAND_ROWS
    xp, xs = x_prompt, x_sample
    rows_p = min(BAND_ROWS, x_prompt.shape[1])
    kp, vp, sp, ks, vs, ss = [], [], [], [], [], []
    for l in range(depth):
        lw = dict(norm_mix=norm_mix[l], w_in=w_in[l], q_norm=q_norm[l], k_norm=k_norm[l],
                  rel_table=rel_table[l], ret_norm=ret_norm[l], w_out=w_out[l],
                  norm_ffn=norm_ffn[l], w_gate=w_gate[l], w_up=w_up[l], w_down=w_down[l])
        xp, k_t, v_t, s_n = _layer(xp, lw, rows_p)
        kp.append(k_t), vp.append(v_t), sp.append(s_n)
        xs, k_t, v_t, s_n = _layer(xs, lw, x_sample.shape[1], cache_a_k[l], cache_a_v[l], state_ret[l])
        ks.append(k_t), vs.append(v_t), ss.append(s_n)
    return (xp, xs, jnp.stack(kp), jnp.stack(vp), jnp.stack(sp),
            jnp.stack(ks), jnp.stack(vs), jnp.stack(ss))
```

```python
import functools

import jax
import jax.numpy as jnp
import numpy as np
from jax import lax
from jax.experimental import pallas as pl
from jax.experimental.pallas import tpu as pltpu

F32 = jnp.float32
BF16 = jnp.bfloat16

D_MODEL = 1024
CHUNK = 64
N_PREV_CHUNKS = 8
BAND_ROWS = N_PREV_CHUNKS * CHUNK
HEADS_A = 8
HEAD_DIM_A = 64
WIDTH_A = HEADS_A * HEAD_DIM_A
HEADS_R = 4
HEAD_DIM_R = 128
WIDTH_R = HEADS_R * HEAD_DIM_R
N_SEG = 7
SEG = 512
REL_CLIP = 2 * CHUNK
ROPE_BASE = 10000.0
RMS_EPS = 1e-6
GN_EPS = 1e-5
NEG_INF = -1e30
LOG2_E = 1.4426950408889634

VMEM_LIMIT_BYTES = 56 * 1024 * 1024
LANES = 128
MXU_DIM = 256


def _resident(shape):
    nd = len(shape)
    return pl.BlockSpec(shape, lambda *_: (0,) * nd, pipeline_mode=pl.Buffered(1))


def _params(n_axes):
    return pltpu.CompilerParams(dimension_semantics=("arbitrary",) * n_axes,
                                vmem_limit_bytes=VMEM_LIMIT_BYTES)


def _proj_kernel(x_ref, g_ref, w_ref, qn_ref, kn_ref, cos_ref, sin_ref, hm_ref,
                 qa_ref, ka_ref, va_ref, qr_ref, kr_ref, vr_ref, gr_ref, kt_ref, vt_ref,
                 *, tail_period, tail_first):
    x = x_ref[...]
    ms = jnp.mean(x * x, axis=-1, keepdims=True)
    h = ((x * lax.rsqrt(ms + RMS_EPS)) * g_ref[...]).astype(BF16)

    def seg(j):
        return jnp.dot(h, w_ref[:, j * SEG:(j + 1) * SEG], preferred_element_type=F32)

    def head_rms(z, gain):
        z2 = (z * z).astype(BF16)
        parts = [jnp.dot(z2[:, c * MXU_DIM:(c + 1) * MXU_DIM], hm_ref[...], preferred_element_type=F32)
                 for c in range(SEG // MXU_DIM)]
        msq = jnp.concatenate(parts, axis=-1)
        return (z * lax.rsqrt(msq + RMS_EPS)) * gain

    def rotary(z):
        outs = []
        for hd in range(HEADS_R):
            zh = z[:, hd * HEAD_DIM_R:(hd + 1) * HEAD_DIM_R]
            outs.append(zh * cos_ref[...] + pltpu.roll(zh, HEAD_DIM_R // 2, axis=1) * sin_ref[...])
        return jnp.concatenate(outs, axis=-1)

    qa = head_rms(seg(0), qn_ref[...]) * (HEAD_DIM_A ** -0.5 * LOG2_E)
    qa_ref[...] = qa.astype(BF16)
    ka = head_rms(seg(1), kn_ref[...])
    ka_ref[...] = ka.astype(BF16)
    va = seg(2)
    va_ref[...] = va.astype(BF16)

    if tail_period == 1:
        kt_ref[...] = ka
        vt_ref[...] = va
    else:
        @pl.when(pl.program_id(0) % tail_period >= tail_first)
        def _():
            kt_ref[...] = ka
            vt_ref[...] = va

    qr_ref[...] = rotary(seg(3)).astype(BF16)
    kr_ref[...] = rotary(seg(4)) * (HEAD_DIM_R ** -0.5)
    vr_ref[...] = seg(5).astype(BF16)
    gr_ref[...] = seg(6)


def _proj(x2d, g_mix, w_in, q_norm, k_norm, seq, tail_rows, tm=512):
    t_rows = x2d.shape[0]
    assert t_rows % tm == 0
    n_tiles = t_rows // tm
    if seq >= tm:
        assert seq % tm == 0 and tail_rows % tm == 0
        tail_period = seq // tm
        tail_first = (seq - tail_rows) // tm
        tail_blocks = tail_rows // tm
        table_rows = seq
    else:
        assert tm % seq == 0 and tail_rows == seq
        tail_period, tail_first, tail_blocks = 1, 0, 1
        table_rows = tm
    n_tab = table_rows // tm

    half = HEAD_DIM_R // 2
    inv = ROPE_BASE ** (-jnp.arange(half, dtype=F32) / half)
    pos = (jnp.arange(table_rows) % seq).astype(F32)
    ang = pos[:, None] * inv
    cos, sin = jnp.cos(ang), jnp.sin(ang)
    cos_t = jnp.concatenate([cos, cos], axis=-1)
    sin_t = jnp.concatenate([-sin, sin], axis=-1)

    lane = np.arange(MXU_DIM)
    head_mean = jnp.asarray((lane[:, None] // HEAD_DIM_A == lane[None, :] // HEAD_DIM_A)
                            .astype(np.float32) / HEAD_DIM_A, dtype=BF16)

    row = lambda i: (i, 0)
    tail_map = lambda i: ((i // tail_period) * tail_blocks
                          + jnp.maximum(i % tail_period - tail_first, 0), 0)
    out_bf = jax.ShapeDtypeStruct((t_rows, SEG), BF16)
    out_f32 = jax.ShapeDtypeStruct((t_rows, SEG), F32)
    tail_shape = jax.ShapeDtypeStruct((t_rows // seq * tail_rows, SEG), F32)
    blk = pl.BlockSpec((tm, SEG), row)
    return pl.pallas_call(
        functools.partial(_proj_kernel, tail_period=tail_period, tail_first=tail_first),
        grid=(n_tiles,),
        in_specs=[
            pl.BlockSpec((tm, D_MODEL), row),
            _resident((1, D_MODEL)),
            _resident((D_MODEL, N_SEG * SEG)),
            _resident((1, SEG)),
            _resident((1, SEG)),
            pl.BlockSpec((tm, HEAD_DIM_R), lambda i: (i % n_tab, 0)),
            pl.BlockSpec((tm, HEAD_DIM_R), lambda i: (i % n_tab, 0)),
            _resident((MXU_DIM, MXU_DIM)),
        ],
        out_specs=[blk, blk, blk, blk, blk, blk, blk,
                   pl.BlockSpec((tm, SEG), tail_map), pl.BlockSpec((tm, SEG), tail_map)],
        out_shape=[out_bf, out_bf, out_bf, out_bf, out_f32, out_bf, out_f32, tail_shape, tail_shape],
        compiler_params=_params(1),
        name="proj",
    )(x2d, g_mix.reshape(1, D_MODEL), w_in.astype(BF16),
      jnp.tile(q_norm, HEADS_A).reshape(1, SEG), jnp.tile(k_norm, HEADS_A).reshape(1, SEG),
      cos_t, sin_t, head_mean)


def _pair_scores(q_pair, k_pair, bias):
    tq = q_pair.shape[0]
    lane = lax.broadcasted_iota(jnp.int32, (tq, LANES), 1)
    zero = jnp.zeros_like(q_pair)
    qz = jnp.concatenate([jnp.where(lane < HEAD_DIM_A, q_pair, zero),
                          jnp.where(lane >= HEAD_DIM_A, q_pair, zero)], axis=0)
    s = lax.dot_general(qz, k_pair, (((1,), (1,)), ((), ())), preferred_element_type=F32)
    return s + bias


def _pair_output(s, v_pair):
    tq = s.shape[0] // 2
    m = jnp.max(s, axis=-1, keepdims=True)
    p = jnp.exp2(s - m)
    denom = jnp.sum(p, axis=-1, keepdims=True)
    r = jnp.dot(p.astype(BF16), v_pair, preferred_element_type=F32) / denom
    lane = lax.broadcasted_iota(jnp.int32, (tq, LANES), 1)
    return jnp.where(lane < HEAD_DIM_A, r[:tq], r[tq:])


def _run_attention_tiles(tiles, bias_ref):
    n_pairs = HEADS_A // 2
    units = [(tile, pair) for tile in tiles for pair in range(n_pairs)]

    def scores(unit):
        tile, pair = unit
        return _pair_scores(tile["q"](pair), tile["k"](pair), bias_ref[pair, :, tile["lo"]:])

    s_next = scores(units[0])
    outs = []
    for idx, (tile, pair) in enumerate(units):
        s_cur = s_next
        if idx + 1 < len(units):
            s_next = scores(units[idx + 1])
        outs.append(_pair_output(s_cur, tile["v"](pair)))
        if pair == n_pairs - 1:
            tile["store"](jnp.concatenate(outs, axis=-1).astype(BF16))
            outs = []


def _attn_prompt_kernel(q_ref, k_ref, v_ref, bias_ref, o_ref, *, tq, tps, n_steps):
    nk = bias_ref.shape[-1]
    n_special = BAND_ROWS // tq
    step = pl.program_id(1)
    lanes = lambda pair: pl.ds(pair * LANES, LANES)

    def make_tile(i, row0, nkv):
        rows = pl.ds(row0, nkv)
        qrows = pl.ds(i * tq, tq)

        def store(o):
            o_ref[0, qrows, :] = o
        return dict(q=lambda pair: q_ref[0, qrows, lanes(pair)],
                    k=lambda pair: k_ref[0, rows, lanes(pair)],
                    v=lambda pair: v_ref[0, rows, lanes(pair)],
                    lo=nk - nkv, store=store)

    for j in range(min(n_special // tps, n_steps)):
        @pl.when(step == j)
        def _(j=j):
            _run_attention_tiles(
                [make_tile(i, 0, (j * tps + i + 1) * tq) for i in range(tps)], bias_ref)

    if n_steps > n_special // tps:
        @pl.when(step >= n_special // tps)
        def _():
            tiles = []
            for i in range(tps):
                row0 = pl.multiple_of((step * tps + i) * tq - BAND_ROWS, tq)
                tiles.append(make_tile(i, row0, nk))
            _run_attention_tiles(tiles, bias_ref)


def _attn_cache_kernel(q_ref, k_ref, v_ref, bias_ref, ck_ref, cv_ref, o_ref, *, bb):
    lanes = lambda pair: pl.ds(pair * LANES, LANES)

    def make_tile(bi):
        def band(cache_ref, new_ref, pair):
            return jnp.concatenate([cache_ref[bi, :, lanes(pair)].astype(BF16),
                                    new_ref[bi, :, lanes(pair)]], axis=0)

        def store(o):
            o_ref[bi] = o
        return dict(q=lambda pair: q_ref[bi, :, lanes(pair)],
                    k=lambda pair: band(ck_ref, k_ref, pair),
                    v=lambda pair: band(cv_ref, v_ref, pair),
                    lo=0, store=store)

    _run_attention_tiles([make_tile(bi) for bi in range(bb)], bias_ref)


def _band_bias(rel_table, chunks_per_tile):
    n_band = (N_PREV_CHUNKS + 1) * CHUNK
    table = rel_table.astype(F32)
    u_min = BAND_ROWS + REL_CLIP - (n_band - 1)
    n_far = CHUNK - 1 + BAND_ROWS - REL_CLIP
    g = jnp.concatenate([table[:, u_min:], jnp.broadcast_to(table[:, -1:], (HEADS_A, n_far))], axis=1)
    r = g[:, ::-1]
    base = jnp.stack([r[:, CHUNK - 1 - q:CHUNK - 1 - q + n_band] for q in range(CHUNK)], axis=1)
    rows = [jnp.pad(base, ((0, 0), (0, 0), (i * CHUNK, (chunks_per_tile - 1 - i) * CHUNK)),
                    constant_values=NEG_INF) for i in range(chunks_per_tile)]
    bias = jnp.concatenate(rows, axis=1) * LOG2_E
    return bias.reshape(HEADS_A // 2, 2 * chunks_per_tile * CHUNK, -1)


def _attention(qa, ka, va, rel_table, cache_k=None, cache_v=None):
    b, seq, _ = qa.shape
    out_shape = jax.ShapeDtypeStruct((b, seq, WIDTH_A), BF16)
    if cache_k is None:
        cq, tps = 2, 2
        tq = cq * CHUNK
        assert seq % (tq * tps) == 0 and (BAND_ROWS // tq) % tps == 0
        n_steps = seq // (tq * tps)
        bias = _band_bias(rel_table, cq)
        tile = pl.BlockSpec((1, tq * tps, WIDTH_A), lambda i, t: (i, t, 0))
        whole = pl.BlockSpec((1, seq, WIDTH_A), lambda i, t: (i, 0, 0))
        return pl.pallas_call(
            functools.partial(_attn_prompt_kernel, tq=tq, tps=tps, n_steps=n_steps),
            grid=(b, n_steps),
            in_specs=[tile, whole, whole, _resident(bias.shape)],
            out_specs=tile,
            out_shape=out_shape,
            compiler_params=_params(2),
            name="band_attention",
        )(qa, ka, va, bias)
    assert seq == CHUNK
    bb = 4
    assert b % bb == 0
    bias = _band_bias(rel_table, 1)
    new = pl.BlockSpec((bb, seq, WIDTH_A), lambda i: (i, 0, 0))
    cache = pl.BlockSpec((bb, BAND_ROWS, WIDTH_A), lambda i: (i, 0, 0))
    return pl.pallas_call(
        functools.partial(_attn_cache_kernel, bb=bb),
        grid=(b // bb,),
        in_specs=[new, new, new, _resident(bias.shape), cache, cache],
        out_specs=new,
        out_shape=out_shape,
        compiler_params=_params(1),
        name="band_attention_cache",
    )(qa, ka, va, bias, cache_k, cache_v)


def _retention_kernel(*refs, n_chunks, has_state):
    if has_state:
        (q_ref, k_ref, v_ref, g_ref, dec_ref, qd_ref, kd_ref, sd_ref, rn_ref, rc_ref, rs_ref,
         s0_ref, o_ref, sout_ref, state) = refs
    else:
        (q_ref, k_ref, v_ref, g_ref, dec_ref, qd_ref, kd_ref, sd_ref, rn_ref, rc_ref, rs_ref,
         o_ref, sout_ref, state) = refs
    c = pl.program_id(1)

    @pl.when(c == 0)
    def _():
        if has_state:
            state[...] = s0_ref[0]
        else:
            state[...] = jnp.zeros(state.shape, F32)

    q = q_ref[0]
    k = k_ref[0]
    v = v_ref[0]
    outs = []
    for hd in range(HEADS_R):
        sl = slice(hd * HEAD_DIM_R, (hd + 1) * HEAD_DIM_R)
        qh, kh, vh = q[:, sl], k[:, sl], v[:, sl]
        s_prev = state[hd]
        scores = lax.dot_general(qh, kh.astype(BF16), (((1,), (1,)), ((), ())),
                                 preferred_element_type=F32) * dec_ref[hd]
        intra = jnp.dot(scores.astype(BF16), vh, preferred_element_type=F32)
        cross = jnp.dot(qh, s_prev.astype(BF16), preferred_element_type=F32) * qd_ref[:, sl]
        k_dec = (kh * kd_ref[:, sl]).astype(BF16)
        state[hd] = sd_ref[hd] * s_prev + lax.dot_general(
            k_dec, vh, (((0,), (0,)), ((), ())), preferred_element_type=F32)
        o = intra + cross
        mu = jnp.mean(o, axis=-1, keepdims=True)
        d = o - mu
        var = jnp.mean(d * d, axis=-1, keepdims=True)
        outs.append(d * lax.rsqrt(var + GN_EPS))
    o_norm = jnp.concatenate(outs, axis=-1) * rn_ref[...]
    g = g_ref[0]
    o_ref[0] = ((g * jax.nn.sigmoid(g)) * o_norm).astype(BF16)

    @pl.when(c == n_chunks - 1)
    def _():
        half = HEAD_DIM_R // 2
        for hd in range(HEADS_R):
            s1 = state[hd, 0:half, :]
            s2 = state[hd, half:HEAD_DIM_R, :]
            sout_ref[0, hd, 0:half, :] = s1 * rc_ref[...] - s2 * rs_ref[...]
            sout_ref[0, hd, half:HEAD_DIM_R, :] = s1 * rs_ref[...] + s2 * rc_ref[...]


def _retention(qr, kr, vr, gr, ret_norm, state0=None, block=256):
    b, seq, _ = qr.shape
    blk = min(block, seq)
    assert seq % blk == 0
    n_chunks = seq // blk
    has_state = state0 is not None

    log_gamma = jnp.log(1.0 - 2.0 ** (-5.0 - jnp.arange(HEADS_R, dtype=F32)))
    idx = jnp.arange(blk, dtype=F32)
    diff = idx[:, None] - idx[None, :]
    decay = jnp.where(diff[None] >= 0,
                      jnp.exp(jnp.maximum(diff, 0.0)[None] * log_gamma[:, None, None]), 0.0)
    lanes = lambda t: jnp.repeat(t, HEAD_DIM_R, axis=-1)
    q_decay = lanes(jnp.exp((idx + 1.0)[:, None] * log_gamma[None, :]))
    k_decay = lanes(jnp.exp((blk - 1.0 - idx)[:, None] * log_gamma[None, :]))
    s_decay = jnp.broadcast_to(jnp.exp(blk * log_gamma)[:, None, None], (HEADS_R, 1, HEAD_DIM_R))

    half = HEAD_DIM_R // 2
    inv = ROPE_BASE ** (-jnp.arange(half, dtype=F32) / half)
    ang = jnp.asarray(-seq, dtype=F32) * inv
    reb_cos = jnp.broadcast_to(jnp.cos(ang)[:, None], (half, HEAD_DIM_R))
    reb_sin = jnp.broadcast_to(jnp.sin(ang)[:, None], (half, HEAD_DIM_R))

    tile = pl.BlockSpec((1, blk, WIDTH_R), lambda i, c: (i, c, 0))
    st = pl.BlockSpec((1, HEADS_R, HEAD_DIM_R, HEAD_DIM_R), lambda i, c: (i, 0, 0, 0))
    in_specs = [tile, tile, tile, tile,
                _resident(decay.shape), _resident(q_decay.shape), _resident(k_decay.shape),
                _resident(s_decay.shape), _resident((1, WIDTH_R)),
                _resident(reb_cos.shape), _resident(reb_sin.shape)]
    args = [qr, kr, vr, gr, decay, q_decay, k_decay, s_decay, ret_norm.reshape(1, WIDTH_R),
            reb_cos, reb_sin]
    if has_state:
        in_specs.append(st)
        args.append(state0)
    return pl.pallas_call(
        functools.partial(_retention_kernel, n_chunks=n_chunks, has_state=has_state),
        grid=(b, n_chunks),
        in_specs=in_specs,
        out_specs=[tile, st],
        out_shape=[jax.ShapeDtypeStruct((b, seq, WIDTH_R), BF16),
                   jax.ShapeDtypeStruct((b, HEADS_R, HEAD_DIM_R, HEAD_DIM_R), F32)],
        scratch_shapes=[pltpu.VMEM((HEADS_R, HEAD_DIM_R, HEAD_DIM_R), F32)],
        compiler_params=_params(2),
        name="retention",
    )(*args)


def _merge_ffn_kernel(x_ref, oa_ref, or_ref, wo_ref, g_ref, wg_ref, wu_ref, wd_ref, y_ref, *, ff_chunk):
    mix = jnp.concatenate([oa_ref[...], or_ref[...]], axis=-1)
    x = x_ref[...] + jnp.dot(mix, wo_ref[...], preferred_element_type=F32)
    ms = jnp.mean(x * x, axis=-1, keepdims=True)
    h = ((x * lax.rsqrt(ms + RMS_EPS)) * g_ref[...]).astype(BF16)
    y = x
    d_ff = wg_ref.shape[1]
    for c in range(d_ff // ff_chunk):
        sl = slice(c * ff_chunk, (c + 1) * ff_chunk)
        gate = jnp.dot(h, wg_ref[:, sl], preferred_element_type=F32)
        up = jnp.dot(h, wu_ref[:, sl], preferred_element_type=F32)
        act = ((gate * jax.nn.sigmoid(gate)) * up).astype(BF16)
        y = y + jnp.dot(act, wd_ref[sl, :], preferred_element_type=F32)
    y_ref[...] = y


def _merge_ffn(x2d, o_a, o_r, w_out, g_ffn, w_gate, w_up, w_down, tm=512, ff_chunk=1408):
    t_rows = x2d.shape[0]
    d_ff = w_gate.shape[1]
    assert t_rows % tm == 0 and d_ff % ff_chunk == 0
    row = lambda i: (i, 0)
    return pl.pallas_call(
        functools.partial(_merge_ffn_kernel, ff_chunk=ff_chunk),
        grid=(t_rows // tm,),
        in_specs=[
            pl.BlockSpec((tm, D_MODEL), row),
            pl.BlockSpec((tm, WIDTH_A), row),
            pl.BlockSpec((tm, WIDTH_R), row),
            _resident((D_MODEL, D_MODEL)),
            _resident((1, D_MODEL)),
            _resident((D_MODEL, d_ff)),
            _resident((D_MODEL, d_ff)),
            _resident((d_ff, D_MODEL)),
        ],
        out_specs=pl.BlockSpec((tm, D_MODEL), row),
        out_shape=jax.ShapeDtypeStruct((t_rows, D_MODEL), F32),
        compiler_params=_params(1),
        name="merge_ffn",
    )(x2d, o_a, o_r, w_out.astype(BF16), g_ffn.reshape(1, D_MODEL),
      w_gate.astype(BF16), w_up.astype(BF16), w_down.astype(BF16))


def _layer(x, lw, tail_rows, cache_k=None, cache_v=None, state0=None):
    b, seq, _ = x.shape
    x2d = x.reshape(b * seq, D_MODEL)
    qa, ka, va, qr, kr, vr, gr, k_tail, v_tail = _proj(
        x2d, lw["norm_mix"], lw["w_in"], lw["q_norm"], lw["k_norm"], seq, tail_rows)
    r3 = lambda t: t.reshape(b, seq, SEG)
    if cache_k is not None:
        cache_k = cache_k.reshape(b, BAND_ROWS, WIDTH_A)
        cache_v = cache_v.reshape(b, BAND_ROWS, WIDTH_A)
    o_a = _attention(r3(qa), r3(ka), r3(va), lw["rel_table"], cache_k, cache_v)
    o_r, s_new = _retention(r3(qr), r3(kr), r3(vr), r3(gr), lw["ret_norm"], state0)
    y = _merge_ffn(x2d, o_a.reshape(b * seq, WIDTH_A), o_r.reshape(b * seq, WIDTH_R),
                   lw["w_out"], lw["norm_ffn"], lw["w_gate"], lw["w_up"], lw["w_down"])
    tails = lambda t: t.reshape(b, tail_rows, HEADS_A, HEAD_DIM_A)
    return y.reshape(b, seq, D_MODEL), tails(k_tail), tails(v_tail), s_new


def kernel(x_prompt, x_sample, cache_a_k, cache_a_v, state_ret, norm_mix, w_in, q_norm, k_norm,
           rel_table, ret_norm, w_out, norm_ffn, w_gate, w_up, w_down):
    depth = w_in.shape[0]
    assert cache_a_k.shape[2] == BAND_ROWS
    xp, xs = x_prompt, x_sample
    rows_p = min(BAND_ROWS, x_prompt.shape[1])
    kp, vp, sp, ks, vs, ss = [], [], [], [], [], []
    for l in range(depth):
        lw = dict(norm_mix=norm_mix[l], w_in=w_in[l], q_norm=q_norm[l], k_norm=k_norm[l],
                  rel_table=rel_table[l], ret_norm=ret_norm[l], w_out=w_out[l],
                  norm_ffn=norm_ffn[l], w_gate=w_gate[l], w_up=w_up[l], w_down=w_down[l])
        xp, k_t, v_t, s_n = _layer(xp, lw, rows_p)
        kp.append(k_t), vp.append(v_t), sp.append(s_n)
        xs, k_t, v_t, s_n = _layer(xs, lw, x_sample.shape[1], cache_a_k[l], cache_a_v[l], state_ret[l])
        ks.append(k_t), vs.append(v_t), ss.append(s_n)
    return (xp, xs, jnp.stack(kp), jnp.stack(vp), jnp.stack(sp),
            jnp.stack(ks), jnp.stack(vs), jnp.stack(ss))
```

```python
import functools

import jax
import jax.numpy as jnp
import numpy as np
from jax import lax
from jax.experimental import pallas as pl
from jax.experimental.pallas import tpu as pltpu

F32 = jnp.float32
BF16 = jnp.bfloat16

D_MODEL = 1024
CHUNK = 64
N_PREV_CHUNKS = 8
BAND_ROWS = N_PREV_CHUNKS * CHUNK
HEADS_A = 8
HEAD_DIM_A = 64
WIDTH_A = HEADS_A * HEAD_DIM_A
HEADS_R = 4
HEAD_DIM_R = 128
WIDTH_R = HEADS_R * HEAD_DIM_R
N_SEG = 7
SEG = 512
REL_CLIP = 2 * CHUNK
ROPE_BASE = 10000.0
RMS_EPS = 1e-6
GN_EPS = 1e-5
NEG_INF = -1e30
LOG2_E = 1.4426950408889634

VMEM_LIMIT_BYTES = 56 * 1024 * 1024
LANES = 128
MXU_DIM = 256


def _resident(shape):
    nd = len(shape)
    return pl.BlockSpec(shape, lambda *_: (0,) * nd, pipeline_mode=pl.Buffered(1))


def _layer_resident(shape, layer):
    nd = len(shape)
    return pl.BlockSpec((None,) + tuple(shape), lambda *_: (layer,) + (0,) * nd,
                        pipeline_mode=pl.Buffered(1))


def _params(n_axes):
    return pltpu.CompilerParams(dimension_semantics=("arbitrary",) * n_axes,
                                vmem_limit_bytes=VMEM_LIMIT_BYTES)


def _proj_kernel(*refs, tail_period, tail_first):
    x_ref, g_ref, w_ref, qn_ref, kn_ref, cos_ref, sin_ref, hm_ref = refs[:8]
    qa_ref, ka_ref, va_ref, qr_ref, kr_ref, vr_ref, gr_ref, kt_ref, vt_ref = refs[-9:]
    x = x_ref[...]
    ms = jnp.mean(x * x, axis=-1, keepdims=True)
    h = ((x * lax.rsqrt(ms + RMS_EPS)) * g_ref[...]).astype(BF16)

    def seg(j):
        return jnp.dot(h, w_ref[:, j * SEG:(j + 1) * SEG], preferred_element_type=F32)

    def head_rms(z, gain):
        z2 = (z * z).astype(BF16)
        parts = [jnp.dot(z2[:, c * MXU_DIM:(c + 1) * MXU_DIM], hm_ref[...], preferred_element_type=F32)
                 for c in range(SEG // MXU_DIM)]
        msq = jnp.concatenate(parts, axis=-1)
        return (z * lax.rsqrt(msq + RMS_EPS)) * gain

    def rotary(z):
        outs = []
        for hd in range(HEADS_R):
            zh = z[:, hd * HEAD_DIM_R:(hd + 1) * HEAD_DIM_R]
            outs.append(zh * cos_ref[...] + pltpu.roll(zh, HEAD_DIM_R // 2, axis=1) * sin_ref[...])
        return jnp.concatenate(outs, axis=-1)

    qa = head_rms(seg(0), qn_ref[...]) * (HEAD_DIM_A ** -0.5 * LOG2_E)
    qa_ref[...] = qa.astype(BF16)
    ka = head_rms(seg(1), kn_ref[...])
    ka_ref[...] = ka.astype(BF16)
    va = seg(2)
    va_ref[...] = va.astype(BF16)

    if tail_period == 1:
        kt_ref[...] = ka
        vt_ref[...] = va
    else:
        @pl.when(pl.program_id(0) % tail_period >= tail_first)
        def _():
            kt_ref[...] = ka
            vt_ref[...] = va

    qr_ref[...] = rotary(seg(3)).astype(BF16)
    kr_ref[...] = rotary(seg(4)) * (HEAD_DIM_R ** -0.5)
    vr_ref[...] = seg(5).astype(BF16)
    gr_ref[...] = seg(6)


def _proj(x2d, prm, layer, seq, tail_rows, prev_tails=None, tm=512):
    t_rows = x2d.shape[0]
    depth = prm["w_in"].shape[0]
    assert t_rows % tm == 0
    n_tiles = t_rows // tm
    if seq >= tm:
        assert seq % tm == 0 and tail_rows % tm == 0
        tail_period = seq // tm
        tail_first = (seq - tail_rows) // tm
        tail_blocks = tail_rows // tm
        table_rows = seq
    else:
        assert tm % seq == 0 and tail_rows == seq
        tail_period, tail_first, tail_blocks = 1, 0, 1
        table_rows = tm
    n_tab = table_rows // tm

    half = HEAD_DIM_R // 2
    inv = ROPE_BASE ** (-jnp.arange(half, dtype=F32) / half)
    pos = (jnp.arange(table_rows) % seq).astype(F32)
    ang = pos[:, None] * inv
    cos, sin = jnp.cos(ang), jnp.sin(ang)
    cos_t = jnp.concatenate([cos, cos], axis=-1)
    sin_t = jnp.concatenate([-sin, sin], axis=-1)

    lane = np.arange(MXU_DIM)
    head_mean = jnp.asarray((lane[:, None] // HEAD_DIM_A == lane[None, :] // HEAD_DIM_A)
                            .astype(np.float32) / HEAD_DIM_A, dtype=BF16)

    row = lambda i: (i, 0)
    layer_tail_blocks = t_rows // seq * tail_rows // tm
    tail_map = lambda i: (layer * layer_tail_blocks + (i // tail_period) * tail_blocks
                          + jnp.maximum(i % tail_period - tail_first, 0), 0)
    out_bf = jax.ShapeDtypeStruct((t_rows, SEG), BF16)
    out_f32 = jax.ShapeDtypeStruct((t_rows, SEG), F32)
    tail_shape = jax.ShapeDtypeStruct((depth * layer_tail_blocks * tm, SEG), F32)
    blk = pl.BlockSpec((tm, SEG), row)
    in_specs = [
        pl.BlockSpec((tm, D_MODEL), row),
        _layer_resident((1, D_MODEL), layer),
        _layer_resident((D_MODEL, N_SEG * SEG), layer),
        _layer_resident((1, SEG), layer),
        _layer_resident((1, SEG), layer),
        pl.BlockSpec((tm, HEAD_DIM_R), lambda i: (i % n_tab, 0)),
        pl.BlockSpec((tm, HEAD_DIM_R), lambda i: (i % n_tab, 0)),
        _resident((MXU_DIM, MXU_DIM)),
    ]
    args = [x2d, prm["norm_mix"], prm["w_in"], prm["q_norm"], prm["k_norm"], cos_t, sin_t, head_mean]
    aliases = {}
    if prev_tails is not None:
        in_specs += [pl.BlockSpec(memory_space=pl.ANY)] * 2
        aliases = {len(args): 7, len(args) + 1: 8}
        args += list(prev_tails)
    return pl.pallas_call(
        functools.partial(_proj_kernel, tail_period=tail_period, tail_first=tail_first),
        grid=(n_tiles,),
        in_specs=in_specs,
        out_specs=[blk, blk, blk, blk, blk, blk, blk,
                   pl.BlockSpec((tm, SEG), tail_map), pl.BlockSpec((tm, SEG), tail_map)],
        out_shape=[out_bf, out_bf, out_bf, out_bf, out_f32, out_bf, out_f32, tail_shape, tail_shape],
        input_output_aliases=aliases,
        compiler_params=_params(1),
        name="proj",
    )(*args)


def _pair_scores(q_pair, k_pair, bias):
    tq = q_pair.shape[0]
    lane = lax.broadcasted_iota(jnp.int32, (tq, LANES), 1)
    zero = jnp.zeros_like(q_pair)
    qz = jnp.concatenate([jnp.where(lane < HEAD_DIM_A, q_pair, zero),
                          jnp.where(lane >= HEAD_DIM_A, q_pair, zero)], axis=0)
    s = lax.dot_general(qz, k_pair, (((1,), (1,)), ((), ())), preferred_element_type=F32)
    return s + bias


def _pair_output(s, v_pair):
    tq = s.shape[0] // 2
    m = jnp.max(s, axis=-1, keepdims=True)
    p = jnp.exp2(s - m)
    denom = jnp.sum(p, axis=-1, keepdims=True)
    r = jnp.dot(p.astype(BF16), v_pair, preferred_element_type=F32) / denom
    lane = lax.broadcasted_iota(jnp.int32, (tq, LANES), 1)
    return jnp.where(lane < HEAD_DIM_A, r[:tq], r[tq:])


def _run_attention_tiles(tiles, bias_ref):
    n_pairs = HEADS_A // 2
    units = [(tile, pair) for tile in tiles for pair in range(n_pairs)]

    def scores(unit):
        tile, pair = unit
        return _pair_scores(tile["q"](pair), tile["k"](pair), bias_ref[pair, :, tile["lo"]:])

    s_next = scores(units[0])
    outs = []
    for idx, (tile, pair) in enumerate(units):
        s_cur = s_next
        if idx + 1 < len(units):
            s_next = scores(units[idx + 1])
        outs.append(_pair_output(s_cur, tile["v"](pair)))
        if pair == n_pairs - 1:
            tile["store"](jnp.concatenate(outs, axis=-1).astype(BF16))
            outs = []


def _attn_prompt_kernel(q_ref, k_ref, v_ref, bias_ref, o_ref, *, tq, tps, n_steps):
    nk = bias_ref.shape[-1]
    n_special = BAND_ROWS // tq
    step = pl.program_id(1)
    lanes = lambda pair: pl.ds(pair * LANES, LANES)

    def make_tile(i, row0, nkv):
        rows = pl.ds(row0, nkv)
        qrows = pl.ds(i * tq, tq)

        def store(o):
            o_ref[0, qrows, :] = o
        return dict(q=lambda pair: q_ref[0, qrows, lanes(pair)],
                    k=lambda pair: k_ref[0, rows, lanes(pair)],
                    v=lambda pair: v_ref[0, rows, lanes(pair)],
                    lo=nk - nkv, store=store)

    for j in range(min(n_special // tps, n_steps)):
        @pl.when(step == j)
        def _(j=j):
            _run_attention_tiles(
                [make_tile(i, 0, (j * tps + i + 1) * tq) for i in range(tps)], bias_ref)

    if n_steps > n_special // tps:
        @pl.when(step >= n_special // tps)
        def _():
            tiles = []
            for i in range(tps):
                row0 = pl.multiple_of((step * tps + i) * tq - BAND_ROWS, tq)
                tiles.append(make_tile(i, row0, nk))
            _run_attention_tiles(tiles, bias_ref)


def _attn_cache_kernel(q_ref, k_ref, v_ref, bias_ref, ck_ref, cv_ref, o_ref, *, bb):
    lanes = lambda pair: pl.ds(pair * LANES, LANES)

    def make_tile(bi):
        def band(cache_ref, new_ref, pair):
            return jnp.concatenate([cache_ref[bi, :, lanes(pair)], new_ref[bi, :, lanes(pair)]], axis=0)

        def store(o):
            o_ref[bi] = o
        return dict(q=lambda pair: q_ref[bi, :, lanes(pair)],
                    k=lambda pair: band(ck_ref, k_ref, pair),
                    v=lambda pair: band(cv_ref, v_ref, pair),
                    lo=0, store=store)

    _run_attention_tiles([make_tile(bi) for bi in range(bb)], bias_ref)


def _band_bias(rel_table, chunks_per_tile):
    n_band = (N_PREV_CHUNKS + 1) * CHUNK
    table = rel_table.astype(F32)
    lead = table.shape[:-1]
    u_min = BAND_ROWS + REL_CLIP - (n_band - 1)
    n_far = CHUNK - 1 + BAND_ROWS - REL_CLIP
    g = jnp.concatenate([table[..., u_min:], jnp.broadcast_to(table[..., -1:], lead + (n_far,))], axis=-1)
    r = jnp.concatenate([g[..., ::-1], jnp.zeros(lead + (1,), F32)], axis=-1)
    period = r.shape[-1]
    tiled = jnp.tile(r, (1,) * len(lead) + (CHUNK + 1,))[..., :CHUNK * (period + 1)]
    hankel = tiled.reshape(lead + (CHUNK, period + 1))[..., :n_band]
    base = hankel[..., ::-1, :]
    pad_lead = ((0, 0),) * (len(lead) + 1)
    rows = [jnp.pad(base, pad_lead + ((i * CHUNK, (chunks_per_tile - 1 - i) * CHUNK),),
                    constant_values=NEG_INF) for i in range(chunks_per_tile)]
    bias = jnp.concatenate(rows, axis=-2) * LOG2_E
    return bias.reshape(lead[:-1] + (HEADS_A // 2, 2 * chunks_per_tile * CHUNK, bias.shape[-1]))


ATTN_CHUNKS_PER_TILE = 2
ATTN_TILES_PER_STEP = 2
ATTN_CACHE_BATCH = 4


def _attention(qa, ka, va, bias, layer, cache_k=None, cache_v=None):
    b, seq, _ = qa.shape
    out_shape = jax.ShapeDtypeStruct((b, seq, WIDTH_A), BF16)
    bias_spec = _layer_resident(bias.shape[1:], layer)
    if cache_k is None:
        tps = ATTN_TILES_PER_STEP
        tq = ATTN_CHUNKS_PER_TILE * CHUNK
        assert seq % (tq * tps) == 0 and (BAND_ROWS // tq) % tps == 0
        n_steps = seq // (tq * tps)
        tile = pl.BlockSpec((1, tq * tps, WIDTH_A), lambda i, t: (i, t, 0))
        whole = pl.BlockSpec((1, seq, WIDTH_A), lambda i, t: (i, 0, 0))
        return pl.pallas_call(
            functools.partial(_attn_prompt_kernel, tq=tq, tps=tps, n_steps=n_steps),
            grid=(b, n_steps),
            in_specs=[tile, whole, whole, bias_spec],
            out_specs=tile,
            out_shape=out_shape,
            compiler_params=_params(2),
            name="band_attention",
        )(qa, ka, va, bias)
    assert seq == CHUNK
    bb = ATTN_CACHE_BATCH
    assert b % bb == 0
    new = pl.BlockSpec((bb, seq, WIDTH_A), lambda i: (i, 0, 0))
    cache = pl.BlockSpec((None, bb, BAND_ROWS, WIDTH_A), lambda i: (layer, i, 0, 0))
    return pl.pallas_call(
        functools.partial(_attn_cache_kernel, bb=bb),
        grid=(b // bb,),
        in_specs=[new, new, new, bias_spec, cache, cache],
        out_specs=new,
        out_shape=out_shape,
        compiler_params=_params(1),
        name="band_attention_cache",
    )(qa, ka, va, bias, cache_k, cache_v)


def _retention_kernel(*refs, n_chunks, has_state):
    if has_state:
        (q_ref, k_ref, v_ref, g_ref, dec_ref, qd_ref, kd_ref, sd_ref, rn_ref, rc_ref, rs_ref,
         s0_ref, o_ref, sout_ref, state) = refs
    else:
        (q_ref, k_ref, v_ref, g_ref, dec_ref, qd_ref, kd_ref, sd_ref, rn_ref, rc_ref, rs_ref,
         o_ref, sout_ref, state) = refs
    c = pl.program_id(1)

    @pl.when(c == 0)
    def _():
        if has_state:
            state[...] = s0_ref[0]
        else:
            state[...] = jnp.zeros(state.shape, F32)

    q = q_ref[0]
    k = k_ref[0]
    v = v_ref[0]
    outs = []
    for hd in range(HEADS_R):
        sl = slice(hd * HEAD_DIM_R, (hd + 1) * HEAD_DIM_R)
        qh, kh, vh = q[:, sl], k[:, sl], v[:, sl]
        s_prev = state[hd]
        scores = lax.dot_general(qh, kh.astype(BF16), (((1,), (1,)), ((), ())),
                                 preferred_element_type=F32) * dec_ref[hd]
        intra = jnp.dot(scores.astype(BF16), vh, preferred_element_type=F32)
        cross = jnp.dot(qh, s_prev.astype(BF16), preferred_element_type=F32) * qd_ref[:, sl]
        k_dec = (kh * kd_ref[:, sl]).astype(BF16)
        state[hd] = sd_ref[hd] * s_prev + lax.dot_general(
            k_dec, vh, (((0,), (0,)), ((), ())), preferred_element_type=F32)
        o = intra + cross
        mu = jnp.mean(o, axis=-1, keepdims=True)
        d = o - mu
        var = jnp.mean(d * d, axis=-1, keepdims=True)
        outs.append(d * lax.rsqrt(var + GN_EPS))
    o_norm = jnp.concatenate(outs, axis=-1) * rn_ref[...]
    g = g_ref[0]
    o_ref[0] = ((g * jax.nn.sigmoid(g)) * o_norm).astype(BF16)

    @pl.when(c == n_chunks - 1)
    def _():
        half = HEAD_DIM_R // 2
        for hd in range(HEADS_R):
            s1 = state[hd, 0:half, :]
            s2 = state[hd, half:HEAD_DIM_R, :]
            sout_ref[0, hd, 0:half, :] = s1 * rc_ref[...] - s2 * rs_ref[...]
            sout_ref[0, hd, half:HEAD_DIM_R, :] = s1 * rs_ref[...] + s2 * rc_ref[...]


def _retention(qr, kr, vr, gr, ret_norm, layer, state0=None, block=256):
    b, seq, _ = qr.shape
    blk = min(block, seq)
    assert seq % blk == 0
    n_chunks = seq // blk
    has_state = state0 is not None

    log_gamma = jnp.log(1.0 - 2.0 ** (-5.0 - jnp.arange(HEADS_R, dtype=F32)))
    idx = jnp.arange(blk, dtype=F32)
    diff = idx[:, None] - idx[None, :]
    decay = jnp.where(diff[None] >= 0,
                      jnp.exp(jnp.maximum(diff, 0.0)[None] * log_gamma[:, None, None]), 0.0)
    lanes = lambda t: jnp.repeat(t, HEAD_DIM_R, axis=-1)
    q_decay = lanes(jnp.exp((idx + 1.0)[:, None] * log_gamma[None, :]))
    k_decay = lanes(jnp.exp((blk - 1.0 - idx)[:, None] * log_gamma[None, :]))
    s_decay = jnp.broadcast_to(jnp.exp(blk * log_gamma)[:, None, None], (HEADS_R, 1, HEAD_DIM_R))

    half = HEAD_DIM_R // 2
    inv = ROPE_BASE ** (-jnp.arange(half, dtype=F32) / half)
    ang = jnp.asarray(-seq, dtype=F32) * inv
    reb_cos = jnp.broadcast_to(jnp.cos(ang)[:, None], (half, HEAD_DIM_R))
    reb_sin = jnp.broadcast_to(jnp.sin(ang)[:, None], (half, HEAD_DIM_R))

    tile = pl.BlockSpec((1, blk, WIDTH_R), lambda i, c: (i, c, 0))
    st = pl.BlockSpec((1, HEADS_R, HEAD_DIM_R, HEAD_DIM_R), lambda i, c: (i, 0, 0, 0))
    in_specs = [tile, tile, tile, tile,
                _resident(decay.shape), _resident(q_decay.shape), _resident(k_decay.shape),
                _resident(s_decay.shape), _layer_resident((1, WIDTH_R), layer),
                _resident(reb_cos.shape), _resident(reb_sin.shape)]
    args = [qr, kr, vr, gr, decay, q_decay, k_decay, s_decay, ret_norm, reb_cos, reb_sin]
    if has_state:
        in_specs.append(pl.BlockSpec((None, 1, HEADS_R, HEAD_DIM_R, HEAD_DIM_R),
                                     lambda i, c: (layer, i, 0, 0, 0)))
        args.append(state0)
    return pl.pallas_call(
        functools.partial(_retention_kernel, n_chunks=n_chunks, has_state=has_state),
        grid=(b, n_chunks),
        in_specs=in_specs,
        out_specs=[tile, st],
        out_shape=[jax.ShapeDtypeStruct((b, seq, WIDTH_R), BF16),
                   jax.ShapeDtypeStruct((b, HEADS_R, HEAD_DIM_R, HEAD_DIM_R), F32)],
        scratch_shapes=[pltpu.VMEM((HEADS_R, HEAD_DIM_R, HEAD_DIM_R), F32)],
        compiler_params=_params(2),
        name="retention",
    )(*args)


def _ff_chunks(d_ff):
    n_tiles = d_ff // MXU_DIM
    first = (n_tiles + 1) // 2 * MXU_DIM
    return [c for c in (first, d_ff - first) if c > 0] if n_tiles > 1 else [d_ff]


def _merge_ffn_kernel(x_ref, oa_ref, or_ref, wo_ref, g_ref, wg_ref, wu_ref, wd_ref, y_ref):
    mix = jnp.concatenate([oa_ref[...], or_ref[...]], axis=-1)
    x = x_ref[...] + jnp.dot(mix, wo_ref[...], preferred_element_type=F32)
    ms = jnp.mean(x * x, axis=-1, keepdims=True)
    h = ((x * lax.rsqrt(ms + RMS_EPS)) * g_ref[...]).astype(BF16)
    y = x
    lo = 0
    for width in _ff_chunks(wg_ref.shape[1]):
        sl = slice(lo, lo + width)
        gate = jnp.dot(h, wg_ref[:, sl], preferred_element_type=F32)
        up = jnp.dot(h, wu_ref[:, sl], preferred_element_type=F32)
        act = ((gate * jax.nn.sigmoid(gate)) * up).astype(BF16)
        y = y + jnp.dot(act, wd_ref[sl, :], preferred_element_type=F32)
        lo += width
    y_ref[...] = y


def _merge_ffn(x2d, o_a, o_r, prm, layer, tm=512):
    t_rows = x2d.shape[0]
    d_ff = prm["w_gate"].shape[-1]
    assert t_rows % tm == 0
    row = lambda i: (i, 0)
    return pl.pallas_call(
        _merge_ffn_kernel,
        grid=(t_rows // tm,),
        in_specs=[
            pl.BlockSpec((tm, D_MODEL), row),
            pl.BlockSpec((tm, WIDTH_A), row),
            pl.BlockSpec((tm, WIDTH_R), row),
            _layer_resident((D_MODEL, D_MODEL), layer),
            _layer_resident((1, D_MODEL), layer),
            _layer_resident((D_MODEL, d_ff), layer),
            _layer_resident((D_MODEL, d_ff), layer),
            _layer_resident((d_ff, D_MODEL), layer),
        ],
        out_specs=pl.BlockSpec((tm, D_MODEL), row),
        out_shape=jax.ShapeDtypeStruct((t_rows, D_MODEL), F32),
        compiler_params=_params(1),
        name="merge_ffn",
    )(x2d, o_a, o_r, prm["w_out"], prm["norm_ffn"], prm["w_gate"], prm["w_up"], prm["w_down"])


def _layer(x, prm, layer, bias, tail_rows, prev_tails, cache_k=None, cache_v=None, state0=None):
    b, seq, _ = x.shape
    x2d = x.reshape(b * seq, D_MODEL)
    qa, ka, va, qr, kr, vr, gr, k_tail, v_tail = _proj(x2d, prm, layer, seq, tail_rows, prev_tails)
    r3 = lambda t: t.reshape(b, seq, SEG)
    o_a = _attention(r3(qa), r3(ka), r3(va), bias, layer, cache_k, cache_v)
    o_r, s_new = _retention(r3(qr), r3(kr), r3(vr), r3(gr), prm["ret_norm"], layer, state0)
    y = _merge_ffn(x2d, o_a.reshape(b * seq, WIDTH_A), o_r.reshape(b * seq, WIDTH_R), prm, layer)
    return y.reshape(b, seq, D_MODEL), (k_tail, v_tail), s_new


def kernel(x_prompt, x_sample, cache_a_k, cache_a_v, state_ret, norm_mix, w_in, q_norm, k_norm,
           rel_table, ret_norm, w_out, norm_ffn, w_gate, w_up, w_down):
    depth = w_in.shape[0]
    bp, sq_p, _ = x_prompt.shape
    bs, sq_s, _ = x_sample.shape
    assert cache_a_k.shape[2] == BAND_ROWS
    rows_p = min(BAND_ROWS, sq_p)

    vec = lambda t: t.reshape(depth, 1, -1)
    prm = dict(
        norm_mix=vec(norm_mix), norm_ffn=vec(norm_ffn), ret_norm=vec(ret_norm),
        q_norm=vec(jnp.tile(q_norm, (1, HEADS_A))), k_norm=vec(jnp.tile(k_norm, (1, HEADS_A))),
        w_in=w_in.astype(BF16), w_out=w_out.astype(BF16),
        w_gate=w_gate.astype(BF16), w_up=w_up.astype(BF16), w_down=w_down.astype(BF16))
    bias_p = _band_bias(rel_table, ATTN_CHUNKS_PER_TILE)
    bias_s = _band_bias(rel_table, 1)
    cache_k = cache_a_k.reshape(depth, bs, BAND_ROWS, WIDTH_A).astype(BF16)
    cache_v = cache_a_v.reshape(depth, bs, BAND_ROWS, WIDTH_A).astype(BF16)

    xp, xs = x_prompt, x_sample
    tails_p = tails_s = None
    sp, ss = [], []
    for l in range(depth):
        xp, tails_p, s_n = _layer(xp, prm, l, bias_p, rows_p, tails_p)
        sp.append(s_n)
        xs, tails_s, s_n = _layer(xs, prm, l, bias_s, sq_s, tails_s, cache_k, cache_v, state_ret)
        ss.append(s_n)
    heads = lambda t, b, rows: t.reshape(depth, b, rows, HEADS_A, HEAD_DIM_A)
    return (xp, xs, heads(tails_p[0], bp, rows_p), heads(tails_p[1], bp, rows_p), jnp.stack(sp),
            heads(tails_s[0], bs, sq_s), heads(tails_s[1], bs, sq_s), jnp.stack(ss))
```

```python
import functools

import jax
import jax.numpy as jnp
import numpy as np
from jax import lax
from jax.experimental import pallas as pl
from jax.experimental.pallas import tpu as pltpu

F32 = jnp.float32
BF16 = jnp.bfloat16

D_MODEL = 1024
CHUNK = 64
N_PREV_CHUNKS = 8
BAND_ROWS = N_PREV_CHUNKS * CHUNK
HEADS_A = 8
HEAD_DIM_A = 64
WIDTH_A = HEADS_A * HEAD_DIM_A
HEADS_R = 4
HEAD_DIM_R = 128
WIDTH_R = HEADS_R * HEAD_DIM_R
N_SEG = 7
SEG = 512
REL_CLIP = 2 * CHUNK
ROPE_BASE = 10000.0
RMS_EPS = 1e-6
GN_EPS = 1e-5
NEG_INF = -1e30
LOG2_E = 1.4426950408889634

VMEM_LIMIT_BYTES = 56 * 1024 * 1024
LANES = 128
MXU_DIM = 256


def _resident(shape):
    nd = len(shape)
    return pl.BlockSpec(shape, lambda *_: (0,) * nd, pipeline_mode=pl.Buffered(1))


def _layer_resident(shape, layer):
    nd = len(shape)
    return pl.BlockSpec((None,) + tuple(shape), lambda *_: (layer,) + (0,) * nd,
                        pipeline_mode=pl.Buffered(1))


def _params(n_axes):
    return pltpu.CompilerParams(dimension_semantics=("arbitrary",) * n_axes,
                                vmem_limit_bytes=VMEM_LIMIT_BYTES)


def _proj_kernel(x_ref, g_ref, w_ref, qn_ref, kn_ref, cos_ref, sin_ref, hm_ref,
                 qa_ref, ka_ref, va_ref, qr_ref, kr_ref, vr_ref, gr_ref, kt_ref, vt_ref,
                 *, tail_period, tail_first):
    x = x_ref[...]
    ms = jnp.mean(x * x, axis=-1, keepdims=True)
    h = ((x * lax.rsqrt(ms + RMS_EPS)) * g_ref[...]).astype(BF16)

    def seg(j):
        return jnp.dot(h, w_ref[:, j * SEG:(j + 1) * SEG], preferred_element_type=F32)

    def head_rms(z, gain):
        z2 = (z * z).astype(BF16)
        parts = [jnp.dot(z2[:, c * MXU_DIM:(c + 1) * MXU_DIM], hm_ref[...], preferred_element_type=F32)
                 for c in range(SEG // MXU_DIM)]
        msq = jnp.concatenate(parts, axis=-1)
        return (z * lax.rsqrt(msq + RMS_EPS)) * gain

    def rotary(z):
        outs = []
        for hd in range(HEADS_R):
            zh = z[:, hd * HEAD_DIM_R:(hd + 1) * HEAD_DIM_R]
            outs.append(zh * cos_ref[...] + pltpu.roll(zh, HEAD_DIM_R // 2, axis=1) * sin_ref[...])
        return jnp.concatenate(outs, axis=-1)

    qa = head_rms(seg(0), qn_ref[...]) * (HEAD_DIM_A ** -0.5 * LOG2_E)
    qa_ref[...] = qa.astype(BF16)
    ka = head_rms(seg(1), kn_ref[...])
    ka_ref[...] = ka.astype(BF16)
    va = seg(2)
    va_ref[...] = va.astype(BF16)

    def write_tails():
        for hd in range(HEADS_A):
            rows = pl.ds(hd, ka.shape[0], stride=HEADS_A)
            cols = slice(hd * HEAD_DIM_A, (hd + 1) * HEAD_DIM_A)
            kt_ref[rows, :] = ka[:, cols]
            vt_ref[rows, :] = va[:, cols]

    if tail_period == 1:
        write_tails()
    else:
        pl.when(pl.program_id(0) % tail_period >= tail_first)(write_tails)

    qr_ref[...] = rotary(seg(3)).astype(BF16)
    kr_ref[...] = rotary(seg(4)) * (HEAD_DIM_R ** -0.5)
    vr_ref[...] = seg(5).astype(BF16)
    gr_ref[...] = seg(6)


def _proj(x2d, prm, layer, seq, tail_rows, tm=512):
    t_rows = x2d.shape[0]
    assert t_rows % tm == 0
    n_tiles = t_rows // tm
    if seq >= tm:
        assert seq % tm == 0 and tail_rows % tm == 0
        tail_period = seq // tm
        tail_first = (seq - tail_rows) // tm
        tail_blocks = tail_rows // tm
        table_rows = seq
    else:
        assert tm % seq == 0 and tail_rows == seq
        tail_period, tail_first, tail_blocks = 1, 0, 1
        table_rows = tm
    n_tab = table_rows // tm

    half = HEAD_DIM_R // 2
    inv = ROPE_BASE ** (-jnp.arange(half, dtype=F32) / half)
    pos = (jnp.arange(table_rows) % seq).astype(F32)
    ang = pos[:, None] * inv
    cos, sin = jnp.cos(ang), jnp.sin(ang)
    cos_t = jnp.concatenate([cos, cos], axis=-1)
    sin_t = jnp.concatenate([-sin, sin], axis=-1)

    lane = np.arange(MXU_DIM)
    head_mean = jnp.asarray((lane[:, None] // HEAD_DIM_A == lane[None, :] // HEAD_DIM_A)
                            .astype(np.float32) / HEAD_DIM_A, dtype=BF16)

    row = lambda i: (i, 0)
    tail_map = lambda i: ((i // tail_period) * tail_blocks
                          + jnp.maximum(i % tail_period - tail_first, 0), 0)
    out_bf = jax.ShapeDtypeStruct((t_rows, SEG), BF16)
    out_f32 = jax.ShapeDtypeStruct((t_rows, SEG), F32)
    tail_shape = jax.ShapeDtypeStruct((t_rows // seq * tail_rows * HEADS_A, HEAD_DIM_A), F32)
    tail_blk = pl.BlockSpec((tm * HEADS_A, HEAD_DIM_A), tail_map)
    blk = pl.BlockSpec((tm, SEG), row)
    in_specs = [
        pl.BlockSpec((tm, D_MODEL), row),
        _layer_resident((1, D_MODEL), layer),
        _layer_resident((D_MODEL, N_SEG * SEG), layer),
        _layer_resident((1, SEG), layer),
        _layer_resident((1, SEG), layer),
        pl.BlockSpec((tm, HEAD_DIM_R), lambda i: (i % n_tab, 0)),
        pl.BlockSpec((tm, HEAD_DIM_R), lambda i: (i % n_tab, 0)),
        _resident((MXU_DIM, MXU_DIM)),
    ]
    return pl.pallas_call(
        functools.partial(_proj_kernel, tail_period=tail_period, tail_first=tail_first),
        grid=(n_tiles,),
        in_specs=in_specs,
        out_specs=[blk, blk, blk, blk, blk, blk, blk, tail_blk, tail_blk],
        out_shape=[out_bf, out_bf, out_bf, out_bf, out_f32, out_bf, out_f32, tail_shape, tail_shape],
        compiler_params=_params(1),
        name="proj",
    )(x2d, prm["norm_mix"], prm["w_in"], prm["q_norm"], prm["k_norm"], cos_t, sin_t, head_mean)


def _pair_scores(q_pair, k_pair, bias):
    tq = q_pair.shape[0]
    lane = lax.broadcasted_iota(jnp.int32, (tq, LANES), 1)
    zero = jnp.zeros_like(q_pair)
    qz = jnp.concatenate([jnp.where(lane < HEAD_DIM_A, q_pair, zero),
                          jnp.where(lane >= HEAD_DIM_A, q_pair, zero)], axis=0)
    s = lax.dot_general(qz, k_pair, (((1,), (1,)), ((), ())), preferred_element_type=F32)
    return s + bias


def _pair_output(s, v_pair):
    tq = s.shape[0] // 2
    m = jnp.max(s, axis=-1, keepdims=True)
    p = jnp.exp2(s - m)
    denom = jnp.sum(p, axis=-1, keepdims=True)
    r = jnp.dot(p.astype(BF16), v_pair, preferred_element_type=F32) / denom
    lane = lax.broadcasted_iota(jnp.int32, (tq, LANES), 1)
    return jnp.where(lane < HEAD_DIM_A, r[:tq], r[tq:])


def _run_attention_tiles(tiles, bias_ref):
    n_pairs = HEADS_A // 2
    units = [(tile, pair) for tile in tiles for pair in range(n_pairs)]

    def scores(unit):
        tile, pair = unit
        return _pair_scores(tile["q"](pair), tile["k"](pair), bias_ref[pair, :, tile["lo"]:])

    s_next = scores(units[0])
    outs = []
    for idx, (tile, pair) in enumerate(units):
        s_cur = s_next
        if idx + 1 < len(units):
            s_next = scores(units[idx + 1])
        outs.append(_pair_output(s_cur, tile["v"](pair)))
        if pair == n_pairs - 1:
            tile["store"](jnp.concatenate(outs, axis=-1).astype(BF16))
            outs = []


def _attn_prompt_kernel(q_ref, k_ref, v_ref, bias_ref, o_ref, *, tq, tps, n_steps):
    nk = bias_ref.shape[-1]
    n_special = BAND_ROWS // tq
    step = pl.program_id(1)
    lanes = lambda pair: pl.ds(pair * LANES, LANES)

    def make_tile(i, row0, nkv):
        rows = pl.ds(row0, nkv)
        qrows = pl.ds(i * tq, tq)

        def store(o):
            o_ref[0, qrows, :] = o
        return dict(q=lambda pair: q_ref[0, qrows, lanes(pair)],
                    k=lambda pair: k_ref[0, rows, lanes(pair)],
                    v=lambda pair: v_ref[0, rows, lanes(pair)],
                    lo=nk - nkv, store=store)

    for j in range(min(n_special // tps, n_steps)):
        @pl.when(step == j)
        def _(j=j):
            _run_attention_tiles(
                [make_tile(i, 0, (j * tps + i + 1) * tq) for i in range(tps)], bias_ref)

    if n_steps > n_special // tps:
        @pl.when(step >= n_special // tps)
        def _():
            tiles = []
            for i in range(tps):
                row0 = pl.multiple_of((step * tps + i) * tq - BAND_ROWS, tq)
                tiles.append(make_tile(i, row0, nk))
            _run_attention_tiles(tiles, bias_ref)


def _attn_cache_kernel(q_ref, k_ref, v_ref, bias_ref, ck_ref, cv_ref, o_ref, *, bb):
    lanes = lambda pair: pl.ds(pair * LANES, LANES)

    def make_tile(bi):
        def band(cache_ref, new_ref, pair):
            heads = [cache_ref[bi, pl.ds(2 * pair + e, BAND_ROWS, stride=HEADS_A), :] for e in range(2)]
            past = jnp.concatenate(heads, axis=-1).astype(BF16)
            return jnp.concatenate([past, new_ref[bi, :, lanes(pair)]], axis=0)

        def store(o):
            o_ref[bi] = o
        return dict(q=lambda pair: q_ref[bi, :, lanes(pair)],
                    k=lambda pair: band(ck_ref, k_ref, pair),
                    v=lambda pair: band(cv_ref, v_ref, pair),
                    lo=0, store=store)

    _run_attention_tiles([make_tile(bi) for bi in range(bb)], bias_ref)


def _band_bias(rel_table, chunks_per_tile):
    n_band = (N_PREV_CHUNKS + 1) * CHUNK
    table = rel_table.astype(F32)
    lead = table.shape[:-1]
    u_min = BAND_ROWS + REL_CLIP - (n_band - 1)
    n_far = CHUNK - 1 + BAND_ROWS - REL_CLIP
    g = jnp.concatenate([table[..., u_min:], jnp.broadcast_to(table[..., -1:], lead + (n_far,))], axis=-1)
    r = jnp.concatenate([g[..., ::-1], jnp.zeros(lead + (1,), F32)], axis=-1)
    period = r.shape[-1]
    tiled = jnp.tile(r, (1,) * len(lead) + (CHUNK + 1,))[..., :CHUNK * (period + 1)]
    hankel = tiled.reshape(lead + (CHUNK, period + 1))[..., :n_band]
    base = hankel[..., ::-1, :]
    pad_lead = ((0, 0),) * (len(lead) + 1)
    rows = [jnp.pad(base, pad_lead + ((i * CHUNK, (chunks_per_tile - 1 - i) * CHUNK),),
                    constant_values=NEG_INF) for i in range(chunks_per_tile)]
    bias = jnp.concatenate(rows, axis=-2) * LOG2_E
    return bias.reshape(lead[:-1] + (HEADS_A // 2, 2 * chunks_per_tile * CHUNK, bias.shape[-1]))


ATTN_CHUNKS_PER_TILE = 2
ATTN_TILES_PER_STEP = 2
ATTN_CACHE_BATCH = 2


def _attention(qa, ka, va, bias, layer, cache_k=None, cache_v=None):
    b, seq, _ = qa.shape
    out_shape = jax.ShapeDtypeStruct((b, seq, WIDTH_A), BF16)
    bias_spec = _layer_resident(bias.shape[1:], layer)
    if cache_k is None:
        tps = ATTN_TILES_PER_STEP
        tq = ATTN_CHUNKS_PER_TILE * CHUNK
        assert seq % (tq * tps) == 0 and (BAND_ROWS // tq) % tps == 0
        n_steps = seq // (tq * tps)
        tile = pl.BlockSpec((1, tq * tps, WIDTH_A), lambda i, t: (i, t, 0))
        whole = pl.BlockSpec((1, seq, WIDTH_A), lambda i, t: (i, 0, 0))
        return pl.pallas_call(
            functools.partial(_attn_prompt_kernel, tq=tq, tps=tps, n_steps=n_steps),
            grid=(b, n_steps),
            in_specs=[tile, whole, whole, bias_spec],
            out_specs=tile,
            out_shape=out_shape,
            compiler_params=_params(2),
            name="band_attention",
        )(qa, ka, va, bias)
    assert seq == CHUNK
    bb = ATTN_CACHE_BATCH
    assert b % bb == 0
    new = pl.BlockSpec((bb, seq, WIDTH_A), lambda i: (i, 0, 0))
    cache = pl.BlockSpec((None, bb, BAND_ROWS * HEADS_A, HEAD_DIM_A), lambda i: (layer, i, 0, 0))
    return pl.pallas_call(
        functools.partial(_attn_cache_kernel, bb=bb),
        grid=(b // bb,),
        in_specs=[new, new, new, bias_spec, cache, cache],
        out_specs=new,
        out_shape=out_shape,
        compiler_params=_params(1),
        name="band_attention_cache",
    )(qa, ka, va, bias, cache_k, cache_v)


def _retention_kernel(*refs, n_chunks, has_state):
    if has_state:
        (q_ref, k_ref, v_ref, g_ref, dec_ref, qd_ref, kd_ref, sd_ref, rn_ref, rc_ref, rs_ref,
         s0_ref, o_ref, sout_ref, state) = refs
    else:
        (q_ref, k_ref, v_ref, g_ref, dec_ref, qd_ref, kd_ref, sd_ref, rn_ref, rc_ref, rs_ref,
         o_ref, sout_ref, state) = refs
    c = pl.program_id(1)

    @pl.when(c == 0)
    def _():
        if has_state:
            state[...] = s0_ref[0]
        else:
            state[...] = jnp.zeros(state.shape, F32)

    q = q_ref[0]
    k = k_ref[0]
    v = v_ref[0]
    outs = []
    for hd in range(HEADS_R):
        sl = slice(hd * HEAD_DIM_R, (hd + 1) * HEAD_DIM_R)
        qh, kh, vh = q[:, sl], k[:, sl], v[:, sl]
        s_prev = state[hd]
        scores = lax.dot_general(qh, kh.astype(BF16), (((1,), (1,)), ((), ())),
                                 preferred_element_type=F32) * dec_ref[hd]
        intra = jnp.dot(scores.astype(BF16), vh, preferred_element_type=F32)
        cross = jnp.dot(qh, s_prev.astype(BF16), preferred_element_type=F32) * qd_ref[:, sl]
        k_dec = (kh * kd_ref[:, sl]).astype(BF16)
        state[hd] = sd_ref[hd] * s_prev + lax.dot_general(
            k_dec, vh, (((0,), (0,)), ((), ())), preferred_element_type=F32)
        o = intra + cross
        mu = jnp.mean(o, axis=-1, keepdims=True)
        d = o - mu
        var = jnp.mean(d * d, axis=-1, keepdims=True)
        outs.append(d * lax.rsqrt(var + GN_EPS))
    o_norm = jnp.concatenate(outs, axis=-1) * rn_ref[...]
    g = g_ref[0]
    o_ref[0] = ((g * jax.nn.sigmoid(g)) * o_norm).astype(BF16)

    @pl.when(c == n_chunks - 1)
    def _():
        half = HEAD_DIM_R // 2
        for hd in range(HEADS_R):
            s1 = state[hd, 0:half, :]
            s2 = state[hd, half:HEAD_DIM_R, :]
            sout_ref[0, hd, 0:half, :] = s1 * rc_ref[...] - s2 * rs_ref[...]
            sout_ref[0, hd, half:HEAD_DIM_R, :] = s1 * rs_ref[...] + s2 * rc_ref[...]


def _retention(qr, kr, vr, gr, ret_norm, layer, state0=None, block=256):
    b, seq, _ = qr.shape
    blk = min(block, seq)
    assert seq % blk == 0
    n_chunks = seq // blk
    has_state = state0 is not None

    log_gamma = jnp.log(1.0 - 2.0 ** (-5.0 - jnp.arange(HEADS_R, dtype=F32)))
    idx = jnp.arange(blk, dtype=F32)
    diff = idx[:, None] - idx[None, :]
    decay = jnp.where(diff[None] >= 0,
                      jnp.exp(jnp.maximum(diff, 0.0)[None] * log_gamma[:, None, None]), 0.0)
    lanes = lambda t: jnp.repeat(t, HEAD_DIM_R, axis=-1)
    q_decay = lanes(jnp.exp((idx + 1.0)[:, None] * log_gamma[None, :]))
    k_decay = lanes(jnp.exp((blk - 1.0 - idx)[:, None] * log_gamma[None, :]))
    s_decay = jnp.broadcast_to(jnp.exp(blk * log_gamma)[:, None, None], (HEADS_R, 1, HEAD_DIM_R))

    half = HEAD_DIM_R // 2
    inv = ROPE_BASE ** (-jnp.arange(half, dtype=F32) / half)
    ang = jnp.asarray(-seq, dtype=F32) * inv
    reb_cos = jnp.broadcast_to(jnp.cos(ang)[:, None], (half, HEAD_DIM_R))
    reb_sin = jnp.broadcast_to(jnp.sin(ang)[:, None], (half, HEAD_DIM_R))

    tile = pl.BlockSpec((1, blk, WIDTH_R), lambda i, c: (i, c, 0))
    st = pl.BlockSpec((1, HEADS_R, HEAD_DIM_R, HEAD_DIM_R), lambda i, c: (i, 0, 0, 0))
    in_specs = [tile, tile, tile, tile,
                _resident(decay.shape), _resident(q_decay.shape), _resident(k_decay.shape),
                _resident(s_decay.shape), _layer_resident((1, WIDTH_R), layer),
                _resident(reb_cos.shape), _resident(reb_sin.shape)]
    args = [qr, kr, vr, gr, decay, q_decay, k_decay, s_decay, ret_norm, reb_cos, reb_sin]
    if has_state:
        in_specs.append(pl.BlockSpec((None, 1, HEADS_R, HEAD_DIM_R, HEAD_DIM_R),
                                     lambda i, c: (layer, i, 0, 0, 0)))
        args.append(state0)
    return pl.pallas_call(
        functools.partial(_retention_kernel, n_chunks=n_chunks, has_state=has_state),
        grid=(b, n_chunks),
        in_specs=in_specs,
        out_specs=[tile, st],
        out_shape=[jax.ShapeDtypeStruct((b, seq, WIDTH_R), BF16),
                   jax.ShapeDtypeStruct((b, HEADS_R, HEAD_DIM_R, HEAD_DIM_R), F32)],
        scratch_shapes=[pltpu.VMEM((HEADS_R, HEAD_DIM_R, HEAD_DIM_R), F32)],
        compiler_params=_params(2),
        name="retention",
    )(*args)


def _ff_chunks(d_ff):
    n_tiles = d_ff // MXU_DIM
    first = (n_tiles + 1) // 2 * MXU_DIM
    return [c for c in (first, d_ff - first) if c > 0] if n_tiles > 1 else [d_ff]


def _merge_ffn_kernel(x_ref, oa_ref, or_ref, wo_ref, g_ref, wg_ref, wu_ref, wd_ref, y_ref):
    mix = jnp.concatenate([oa_ref[...], or_ref[...]], axis=-1)
    x = x_ref[...] + jnp.dot(mix, wo_ref[...], preferred_element_type=F32)
    ms = jnp.mean(x * x, axis=-1, keepdims=True)
    h = ((x * lax.rsqrt(ms + RMS_EPS)) * g_ref[...]).astype(BF16)
    y = x
    lo = 0
    for width in _ff_chunks(wg_ref.shape[1]):
        sl = slice(lo, lo + width)
        gate = jnp.dot(h, wg_ref[:, sl], preferred_element_type=F32)
        up = jnp.dot(h, wu_ref[:, sl], preferred_element_type=F32)
        act = ((gate * jax.nn.sigmoid(gate)) * up).astype(BF16)
        y = y + jnp.dot(act, wd_ref[sl, :], preferred_element_type=F32)
        lo += width
    y_ref[...] = y


def _merge_ffn(x2d, o_a, o_r, prm, layer, tm=512):
    t_rows = x2d.shape[0]
    d_ff = prm["w_gate"].shape[-1]
    assert t_rows % tm == 0
    row = lambda i: (i, 0)
    return pl.pallas_call(
        _merge_ffn_kernel,
        grid=(t_rows // tm,),
        in_specs=[
            pl.BlockSpec((tm, D_MODEL), row),
            pl.BlockSpec((tm, WIDTH_A), row),
            pl.BlockSpec((tm, WIDTH_R), row),
            _layer_resident((D_MODEL, D_MODEL), layer),
            _layer_resident((1, D_MODEL), layer),
            _layer_resident((D_MODEL, d_ff), layer),
            _layer_resident((D_MODEL, d_ff), layer),
            _layer_resident((d_ff, D_MODEL), layer),
        ],
        out_specs=pl.BlockSpec((tm, D_MODEL), row),
        out_shape=jax.ShapeDtypeStruct((t_rows, D_MODEL), F32),
        compiler_params=_params(1),
        name="merge_ffn",
    )(x2d, o_a, o_r, prm["w_out"], prm["norm_ffn"], prm["w_gate"], prm["w_up"], prm["w_down"])


def _layer(x, prm, layer, bias, tail_rows, cache_k=None, cache_v=None, state0=None):
    b, seq, _ = x.shape
    x2d = x.reshape(b * seq, D_MODEL)
    qa, ka, va, qr, kr, vr, gr, k_tail, v_tail = _proj(x2d, prm, layer, seq, tail_rows)
    r3 = lambda t: t.reshape(b, seq, SEG)
    o_a = _attention(r3(qa), r3(ka), r3(va), bias, layer, cache_k, cache_v)
    o_r, s_new = _retention(r3(qr), r3(kr), r3(vr), r3(gr), prm["ret_norm"], layer, state0)
    y = _merge_ffn(x2d, o_a.reshape(b * seq, WIDTH_A), o_r.reshape(b * seq, WIDTH_R), prm, layer)
    heads = lambda t: t.reshape(b, tail_rows, HEADS_A, HEAD_DIM_A)
    return y.reshape(b, seq, D_MODEL), heads(k_tail), heads(v_tail), s_new


def kernel(x_prompt, x_sample, cache_a_k, cache_a_v, state_ret, norm_mix, w_in, q_norm, k_norm,
           rel_table, ret_norm, w_out, norm_ffn, w_gate, w_up, w_down):
    depth = w_in.shape[0]
    bp, sq_p, _ = x_prompt.shape
    bs, sq_s, _ = x_sample.shape
    assert cache_a_k.shape[2] == BAND_ROWS
    rows_p = min(BAND_ROWS, sq_p)

    vec = lambda t: t.reshape(depth, 1, -1)
    prm = dict(
        norm_mix=vec(norm_mix), norm_ffn=vec(norm_ffn), ret_norm=vec(ret_norm),
        q_norm=vec(jnp.tile(q_norm, (1, HEADS_A))), k_norm=vec(jnp.tile(k_norm, (1, HEADS_A))),
        w_in=w_in.astype(BF16), w_out=w_out.astype(BF16),
        w_gate=w_gate.astype(BF16), w_up=w_up.astype(BF16), w_down=w_down.astype(BF16))
    bias_p = _band_bias(rel_table, ATTN_CHUNKS_PER_TILE)
    bias_s = _band_bias(rel_table, 1)
    cache_k = cache_a_k.reshape(depth, bs, BAND_ROWS * HEADS_A, HEAD_DIM_A)
    cache_v = cache_a_v.reshape(depth, bs, BAND_ROWS * HEADS_A, HEAD_DIM_A)

    xp, xs = x_prompt, x_sample
    kp, vp, sp, ks, vs, ss = [], [], [], [], [], []
    for l in range(depth):
        xp, k_t, v_t, s_n = _layer(xp, prm, l, bias_p, rows_p)
        kp.append(k_t), vp.append(v_t), sp.append(s_n)
        xs, k_t, v_t, s_n = _layer(xs, prm, l, bias_s, sq_s, cache_k, cache_v, state_ret)
        ks.append(k_t), vs.append(v_t), ss.append(s_n)
    return (xp, xs, jnp.stack(kp), jnp.stack(vp), jnp.stack(sp),
            jnp.stack(ks), jnp.stack(vs), jnp.stack(ss))
```

```python
import functools

import jax
import jax.numpy as jnp
import numpy as np
from jax import lax
from jax.experimental import pallas as pl
from jax.experimental.pallas import tpu as pltpu

F32 = jnp.float32
BF16 = jnp.bfloat16

D_MODEL = 1024
CHUNK = 64
N_PREV_CHUNKS = 8
BAND_ROWS = N_PREV_CHUNKS * CHUNK
HEADS_A = 8
HEAD_DIM_A = 64
WIDTH_A = HEADS_A * HEAD_DIM_A
HEADS_R = 4
HEAD_DIM_R = 128
WIDTH_R = HEADS_R * HEAD_DIM_R
N_SEG = 7
SEG = 512
REL_CLIP = 2 * CHUNK
ROPE_BASE = 10000.0
RMS_EPS = 1e-6
GN_EPS = 1e-5
NEG_INF = -1e30
LOG2_E = 1.4426950408889634

VMEM_LIMIT_BYTES = 56 * 1024 * 1024
LANES = 128
MXU_DIM = 256


def _resident(shape):
    nd = len(shape)
    return pl.BlockSpec(shape, lambda *_: (0,) * nd, pipeline_mode=pl.Buffered(1))


def _layer_resident(shape, layer):
    nd = len(shape)
    return pl.BlockSpec((None,) + tuple(shape), lambda *_: (layer,) + (0,) * nd,
                        pipeline_mode=pl.Buffered(1))


def _params(n_axes):
    return pltpu.CompilerParams(dimension_semantics=("arbitrary",) * n_axes,
                                vmem_limit_bytes=VMEM_LIMIT_BYTES)


def _proj_kernel(x_ref, g_ref, w_ref, qn_ref, kn_ref, cos_ref, sin_ref, hm_ref,
                 qa_ref, ka_ref, va_ref, qr_ref, kr_ref, vr_ref, gr_ref, kt_ref, vt_ref,
                 *, tail_period, tail_first):
    x = x_ref[...]
    ms = jnp.mean(x * x, axis=-1, keepdims=True)
    h = ((x * lax.rsqrt(ms + RMS_EPS)) * g_ref[...]).astype(BF16)

    def seg(j):
        return jnp.dot(h, w_ref[:, j * SEG:(j + 1) * SEG], preferred_element_type=F32)

    def head_rms(z, gain):
        z2 = (z * z).astype(BF16)
        parts = [jnp.dot(z2[:, c * MXU_DIM:(c + 1) * MXU_DIM], hm_ref[...], preferred_element_type=F32)
                 for c in range(SEG // MXU_DIM)]
        msq = jnp.concatenate(parts, axis=-1)
        return (z * lax.rsqrt(msq + RMS_EPS)) * gain

    def rotary(z):
        outs = []
        for hd in range(HEADS_R):
            zh = z[:, hd * HEAD_DIM_R:(hd + 1) * HEAD_DIM_R]
            outs.append(zh * cos_ref[...] + pltpu.roll(zh, HEAD_DIM_R // 2, axis=1) * sin_ref[...])
        return jnp.concatenate(outs, axis=-1)

    qa = head_rms(seg(0), qn_ref[...]) * (HEAD_DIM_A ** -0.5 * LOG2_E)
    qa_ref[...] = qa.astype(BF16)
    ka = head_rms(seg(1), kn_ref[...])
    ka_ref[...] = ka.astype(BF16)
    va = seg(2)
    va_ref[...] = va.astype(BF16)

    def write_tails():
        if len(kt_ref.shape) == 3:
            kt_ref[0] = ka.T
            vt_ref[0] = va.T
            return
        for hd in range(HEADS_A):
            rows = pl.ds(hd, ka.shape[0], stride=HEADS_A)
            cols = slice(hd * HEAD_DIM_A, (hd + 1) * HEAD_DIM_A)
            kt_ref[rows, :] = ka[:, cols]
            vt_ref[rows, :] = va[:, cols]

    if tail_period == 1:
        write_tails()
    else:
        pl.when(pl.program_id(0) % tail_period >= tail_first)(write_tails)

    qr_ref[...] = rotary(seg(3)).astype(BF16)
    kr_ref[...] = rotary(seg(4)) * (HEAD_DIM_R ** -0.5)
    vr_ref[...] = seg(5).astype(BF16)
    gr_ref[...] = seg(6)


def _proj(x2d, prm, layer, seq, tail_rows, tm=512):
    t_rows = x2d.shape[0]
    assert t_rows % tm == 0
    n_tiles = t_rows // tm
    if seq >= tm:
        assert seq % tm == 0 and tail_rows % tm == 0
        tail_period = seq // tm
        tail_first = (seq - tail_rows) // tm
        tail_blocks = tail_rows // tm
        table_rows = seq
    else:
        assert tm % seq == 0 and tail_rows == seq
        tail_period, tail_first, tail_blocks = 1, 0, 1
        table_rows = tm
    n_tab = table_rows // tm

    half = HEAD_DIM_R // 2
    inv = ROPE_BASE ** (-jnp.arange(half, dtype=F32) / half)
    pos = (jnp.arange(table_rows) % seq).astype(F32)
    ang = pos[:, None] * inv
    cos, sin = jnp.cos(ang), jnp.sin(ang)
    cos_t = jnp.concatenate([cos, cos], axis=-1)
    sin_t = jnp.concatenate([-sin, sin], axis=-1)

    lane = np.arange(MXU_DIM)
    head_mean = jnp.asarray((lane[:, None] // HEAD_DIM_A == lane[None, :] // HEAD_DIM_A)
                            .astype(np.float32) / HEAD_DIM_A, dtype=BF16)

    row = lambda i: (i, 0)
    out_bf = jax.ShapeDtypeStruct((t_rows, SEG), BF16)
    out_f32 = jax.ShapeDtypeStruct((t_rows, SEG), F32)
    if seq >= tm:
        tail_shape = jax.ShapeDtypeStruct((t_rows // seq, SEG, tail_rows), F32)
        tail_blk = pl.BlockSpec((1, SEG, tm), lambda i: (
            i // tail_period, 0, jnp.maximum(i % tail_period - tail_first, 0)))
    else:
        tail_shape = jax.ShapeDtypeStruct((t_rows * HEADS_A, HEAD_DIM_A), F32)
        tail_blk = pl.BlockSpec((tm * HEADS_A, HEAD_DIM_A), row)
    blk = pl.BlockSpec((tm, SEG), row)
    in_specs = [
        pl.BlockSpec((tm, D_MODEL), row),
        _layer_resident((1, D_MODEL), layer),
        _layer_resident((D_MODEL, N_SEG * SEG), layer),
        _layer_resident((1, SEG), layer),
        _layer_resident((1, SEG), layer),
        pl.BlockSpec((tm, HEAD_DIM_R), lambda i: (i % n_tab, 0)),
        pl.BlockSpec((tm, HEAD_DIM_R), lambda i: (i % n_tab, 0)),
        _resident((MXU_DIM, MXU_DIM)),
    ]
    return pl.pallas_call(
        functools.partial(_proj_kernel, tail_period=tail_period, tail_first=tail_first),
        grid=(n_tiles,),
        in_specs=in_specs,
        out_specs=[blk, blk, blk, blk, blk, blk, blk, tail_blk, tail_blk],
        out_shape=[out_bf, out_bf, out_bf, out_bf, out_f32, out_bf, out_f32, tail_shape, tail_shape],
        compiler_params=_params(1),
        name="proj",
    )(x2d, prm["norm_mix"], prm["w_in"], prm["q_norm"], prm["k_norm"], cos_t, sin_t, head_mean)


NT_DIMS = (((1,), (1,)), ((), ()))


def _pair_scores(tile, pair, bias_ref):
    q_pair = tile["q"](pair)
    tq = q_pair.shape[0]
    lane = lax.broadcasted_iota(jnp.int32, (tq, LANES), 1)
    zero = jnp.zeros_like(q_pair)
    qz = jnp.concatenate([jnp.where(lane < HEAD_DIM_A, q_pair, zero),
                          jnp.where(lane >= HEAD_DIM_A, q_pair, zero)], axis=0)
    return tile["qk"](pair, qz) + bias_ref[pair, :, tile["lo"]:]


def _pair_output(tile, pair, s):
    tq = s.shape[0] // 2
    m = jnp.max(s, axis=-1, keepdims=True)
    p = jnp.exp2(s - m)
    denom = jnp.sum(p, axis=-1, keepdims=True)
    r = tile["pv"](pair, p.astype(BF16)) / denom
    lane = lax.broadcasted_iota(jnp.int32, (tq, LANES), 1)
    return jnp.where(lane < HEAD_DIM_A, r[:tq], r[tq:])


def _run_attention_tiles(tiles, bias_ref):
    n_pairs = HEADS_A // 2
    units = [(tile, pair) for tile in tiles for pair in range(n_pairs)]
    s_next = _pair_scores(*units[0], bias_ref)
    outs = []
    for idx, (tile, pair) in enumerate(units):
        s_cur = s_next
        if idx + 1 < len(units):
            s_next = _pair_scores(*units[idx + 1], bias_ref)
        outs.append(_pair_output(tile, pair, s_cur))
        if pair == n_pairs - 1:
            tile["store"](jnp.concatenate(outs, axis=-1).astype(BF16))
            outs = []


def _attn_prompt_kernel(q_ref, k_ref, v_ref, bias_ref, o_ref, *, tq, tps, n_steps):
    nk = bias_ref.shape[-1]
    n_special = BAND_ROWS // tq
    step = pl.program_id(1)
    lanes = lambda pair: pl.ds(pair * LANES, LANES)

    def make_tile(i, row0, nkv):
        rows = pl.ds(row0, nkv)
        qrows = pl.ds(i * tq, tq)

        def store(o):
            o_ref[0, qrows, :] = o

        def qk(pair, qz):
            return lax.dot_general(qz, k_ref[0, rows, lanes(pair)], NT_DIMS, preferred_element_type=F32)

        def pv(pair, p):
            return jnp.dot(p, v_ref[0, rows, lanes(pair)], preferred_element_type=F32)
        return dict(q=lambda pair: q_ref[0, qrows, lanes(pair)], qk=qk, pv=pv, lo=nk - nkv, store=store)

    for j in range(min(n_special // tps, n_steps)):
        @pl.when(step == j)
        def _(j=j):
            _run_attention_tiles(
                [make_tile(i, 0, (j * tps + i + 1) * tq) for i in range(tps)], bias_ref)

    if n_steps > n_special // tps:
        @pl.when(step >= n_special // tps)
        def _():
            tiles = []
            for i in range(tps):
                row0 = pl.multiple_of((step * tps + i) * tq - BAND_ROWS, tq)
                tiles.append(make_tile(i, row0, nk))
            _run_attention_tiles(tiles, bias_ref)


def _attn_cache_kernel(q_ref, k_ref, v_ref, bias_ref, ck_ref, cv_ref, o_ref, *, bb):
    lanes = lambda pair: pl.ds(pair * LANES, LANES)

    def make_tile(bi):
        past = lambda ref, pair: ref[bi, lanes(pair), :].astype(BF16)

        def store(o):
            o_ref[bi] = o

        def qk(pair, qz):
            s_past = jnp.dot(qz, past(ck_ref, pair), preferred_element_type=F32)
            s_new = lax.dot_general(qz, k_ref[bi, :, lanes(pair)], NT_DIMS, preferred_element_type=F32)
            return jnp.concatenate([s_past, s_new], axis=-1)

        def pv(pair, p):
            return (lax.dot_general(p[:, :BAND_ROWS], past(cv_ref, pair), NT_DIMS, preferred_element_type=F32)
                    + jnp.dot(p[:, BAND_ROWS:], v_ref[bi, :, lanes(pair)], preferred_element_type=F32))
        return dict(q=lambda pair: q_ref[bi, :, lanes(pair)], qk=qk, pv=pv, lo=0, store=store)

    _run_attention_tiles([make_tile(bi) for bi in range(bb)], bias_ref)


def _band_bias(rel_table, chunks_per_tile):
    n_band = (N_PREV_CHUNKS + 1) * CHUNK
    table = rel_table.astype(F32)
    lead = table.shape[:-1]
    u_min = BAND_ROWS + REL_CLIP - (n_band - 1)
    n_far = CHUNK - 1 + BAND_ROWS - REL_CLIP
    g = jnp.concatenate([table[..., u_min:], jnp.broadcast_to(table[..., -1:], lead + (n_far,))], axis=-1)
    r = jnp.concatenate([g[..., ::-1], jnp.zeros(lead + (1,), F32)], axis=-1)
    period = r.shape[-1]
    tiled = jnp.tile(r, (1,) * len(lead) + (CHUNK + 1,))[..., :CHUNK * (period + 1)]
    hankel = tiled.reshape(lead + (CHUNK, period + 1))[..., :n_band]
    base = hankel[..., ::-1, :]
    pad_lead = ((0, 0),) * (len(lead) + 1)
    rows = [jnp.pad(base, pad_lead + ((i * CHUNK, (chunks_per_tile - 1 - i) * CHUNK),),
                    constant_values=NEG_INF) for i in range(chunks_per_tile)]
    bias = jnp.concatenate(rows, axis=-2) * LOG2_E
    return bias.reshape(lead[:-1] + (HEADS_A // 2, 2 * chunks_per_tile * CHUNK, bias.shape[-1]))


ATTN_CHUNKS_PER_TILE = 2
ATTN_TILES_PER_STEP = 2
ATTN_CACHE_BATCH = 4


def _attention(qa, ka, va, bias, layer, cache_k=None, cache_v=None):
    b, seq, _ = qa.shape
    out_shape = jax.ShapeDtypeStruct((b, seq, WIDTH_A), BF16)
    bias_spec = _layer_resident(bias.shape[1:], layer)
    if cache_k is None:
        tps = ATTN_TILES_PER_STEP
        tq = ATTN_CHUNKS_PER_TILE * CHUNK
        assert seq % (tq * tps) == 0 and (BAND_ROWS // tq) % tps == 0
        n_steps = seq // (tq * tps)
        tile = pl.BlockSpec((1, tq * tps, WIDTH_A), lambda i, t: (i, t, 0))
        whole = pl.BlockSpec((1, seq, WIDTH_A), lambda i, t: (i, 0, 0))
        return pl.pallas_call(
            functools.partial(_attn_prompt_kernel, tq=tq, tps=tps, n_steps=n_steps),
            grid=(b, n_steps),
            in_specs=[tile, whole, whole, bias_spec],
            out_specs=tile,
            out_shape=out_shape,
            compiler_params=_params(2),
            name="band_attention",
        )(qa, ka, va, bias)
    assert seq == CHUNK
    bb = ATTN_CACHE_BATCH
    assert b % bb == 0
    new = pl.BlockSpec((bb, seq, WIDTH_A), lambda i: (i, 0, 0))
    cache = pl.BlockSpec((None, bb, WIDTH_A, BAND_ROWS), lambda i: (layer, i, 0, 0))
    return pl.pallas_call(
        functools.partial(_attn_cache_kernel, bb=bb),
        grid=(b // bb,),
        in_specs=[new, new, new, bias_spec, cache, cache],
        out_specs=new,
        out_shape=out_shape,
        compiler_params=_params(1),
        name="band_attention_cache",
    )(qa, ka, va, bias, cache_k, cache_v)


def _retention_kernel(*refs, n_chunks, has_state):
    if has_state:
        (q_ref, k_ref, v_ref, g_ref, dec_ref, qd_ref, kd_ref, sd_ref, rn_ref, rc_ref, rs_ref,
         s0_ref, o_ref, sout_ref, state) = refs
    else:
        (q_ref, k_ref, v_ref, g_ref, dec_ref, qd_ref, kd_ref, sd_ref, rn_ref, rc_ref, rs_ref,
         o_ref, sout_ref, state) = refs
    c = pl.program_id(1)

    @pl.when(c == 0)
    def _():
        if has_state:
            state[...] = s0_ref[0]
        else:
            state[...] = jnp.zeros(state.shape, F32)

    q = q_ref[0]
    k = k_ref[0]
    v = v_ref[0]
    outs = []
    for hd in range(HEADS_R):
        sl = slice(hd * HEAD_DIM_R, (hd + 1) * HEAD_DIM_R)
        qh, kh, vh = q[:, sl], k[:, sl], v[:, sl]
        s_prev = state[hd]
        scores = lax.dot_general(qh, kh.astype(BF16), (((1,), (1,)), ((), ())),
                                 preferred_element_type=F32) * dec_ref[hd]
        intra = jnp.dot(scores.astype(BF16), vh, preferred_element_type=F32)
        cross = jnp.dot(qh, s_prev.astype(BF16), preferred_element_type=F32) * qd_ref[:, sl]
        k_dec = (kh * kd_ref[:, sl]).astype(BF16)
        state[hd] = sd_ref[hd] * s_prev + lax.dot_general(
            k_dec, vh, (((0,), (0,)), ((), ())), preferred_element_type=F32)
        o = intra + cross
        mu = jnp.mean(o, axis=-1, keepdims=True)
        d = o - mu
        var = jnp.mean(d * d, axis=-1, keepdims=True)
        outs.append(d * lax.rsqrt(var + GN_EPS))
    o_norm = jnp.concatenate(outs, axis=-1) * rn_ref[...]
    g = g_ref[0]
    o_ref[0] = ((g * jax.nn.sigmoid(g)) * o_norm).astype(BF16)

    @pl.when(c == n_chunks - 1)
    def _():
        half = HEAD_DIM_R // 2
        for hd in range(HEADS_R):
            s1 = state[hd, 0:half, :]
            s2 = state[hd, half:HEAD_DIM_R, :]
            sout_ref[0, hd, 0:half, :] = s1 * rc_ref[...] - s2 * rs_ref[...]
            sout_ref[0, hd, half:HEAD_DIM_R, :] = s1 * rs_ref[...] + s2 * rc_ref[...]


def _retention(qr, kr, vr, gr, ret_norm, layer, state0=None, block=256):
    b, seq, _ = qr.shape
    blk = min(block, seq)
    assert seq % blk == 0
    n_chunks = seq // blk
    has_state = state0 is not None

    log_gamma = jnp.log(1.0 - 2.0 ** (-5.0 - jnp.arange(HEADS_R, dtype=F32)))
    idx = jnp.arange(blk, dtype=F32)
    diff = idx[:, None] - idx[None, :]
    decay = jnp.where(diff[None] >= 0,
                      jnp.exp(jnp.maximum(diff, 0.0)[None] * log_gamma[:, None, None]), 0.0)
    lanes = lambda t: jnp.repeat(t, HEAD_DIM_R, axis=-1)
    q_decay = lanes(jnp.exp((idx + 1.0)[:, None] * log_gamma[None, :]))
    k_decay = lanes(jnp.exp((blk - 1.0 - idx)[:, None] * log_gamma[None, :]))
    s_decay = jnp.broadcast_to(jnp.exp(blk * log_gamma)[:, None, None], (HEADS_R, 1, HEAD_DIM_R))

    half = HEAD_DIM_R // 2
    inv = ROPE_BASE ** (-jnp.arange(half, dtype=F32) / half)
    ang = jnp.asarray(-seq, dtype=F32) * inv
    reb_cos = jnp.broadcast_to(jnp.cos(ang)[:, None], (half, HEAD_DIM_R))
    reb_sin = jnp.broadcast_to(jnp.sin(ang)[:, None], (half, HEAD_DIM_R))

    tile = pl.BlockSpec((1, blk, WIDTH_R), lambda i, c: (i, c, 0))
    st = pl.BlockSpec((1, HEADS_R, HEAD_DIM_R, HEAD_DIM_R), lambda i, c: (i, 0, 0, 0))
    in_specs = [tile, tile, tile, tile,
                _resident(decay.shape), _resident(q_decay.shape), _resident(k_decay.shape),
                _resident(s_decay.shape), _layer_resident((1, WIDTH_R), layer),
                _resident(reb_cos.shape), _resident(reb_sin.shape)]
    args = [qr, kr, vr, gr, decay, q_decay, k_decay, s_decay, ret_norm, reb_cos, reb_sin]
    if has_state:
        in_specs.append(pl.BlockSpec((None, 1, HEADS_R, HEAD_DIM_R, HEAD_DIM_R),
                                     lambda i, c: (layer, i, 0, 0, 0)))
        args.append(state0)
    return pl.pallas_call(
        functools.partial(_retention_kernel, n_chunks=n_chunks, has_state=has_state),
        grid=(b, n_chunks),
        in_specs=in_specs,
        out_specs=[tile, st],
        out_shape=[jax.ShapeDtypeStruct((b, seq, WIDTH_R), BF16),
                   jax.ShapeDtypeStruct((b, HEADS_R, HEAD_DIM_R, HEAD_DIM_R), F32)],
        scratch_shapes=[pltpu.VMEM((HEADS_R, HEAD_DIM_R, HEAD_DIM_R), F32)],
        compiler_params=_params(2),
        name="retention",
    )(*args)


def _ff_chunks(d_ff):
    n_tiles = d_ff // MXU_DIM
    first = (n_tiles + 1) // 2 * MXU_DIM
    return [c for c in (first, d_ff - first) if c > 0] if n_tiles > 1 else [d_ff]


def _merge_ffn_kernel(x_ref, oa_ref, or_ref, wo_ref, g_ref, wg_ref, wu_ref, wd_ref, y_ref):
    mix = jnp.concatenate([oa_ref[...], or_ref[...]], axis=-1)
    x = x_ref[...] + jnp.dot(mix, wo_ref[...], preferred_element_type=F32)
    ms = jnp.mean(x * x, axis=-1, keepdims=True)
    h = ((x * lax.rsqrt(ms + RMS_EPS)) * g_ref[...]).astype(BF16)
    y = x
    lo = 0
    for width in _ff_chunks(wg_ref.shape[1]):
        sl = slice(lo, lo + width)
        gate = jnp.dot(h, wg_ref[:, sl], preferred_element_type=F32)
        up = jnp.dot(h, wu_ref[:, sl], preferred_element_type=F32)
        act = ((gate * jax.nn.sigmoid(gate)) * up).astype(BF16)
        y = y + jnp.dot(act, wd_ref[sl, :], preferred_element_type=F32)
        lo += width
    y_ref[...] = y


def _merge_ffn(x2d, o_a, o_r, prm, layer, tm=512):
    t_rows = x2d.shape[0]
    d_ff = prm["w_gate"].shape[-1]
    assert t_rows % tm == 0
    row = lambda i: (i, 0)
    return pl.pallas_call(
        _merge_ffn_kernel,
        grid=(t_rows // tm,),
        in_specs=[
            pl.BlockSpec((tm, D_MODEL), row),
            pl.BlockSpec((tm, WIDTH_A), row),
            pl.BlockSpec((tm, WIDTH_R), row),
            _layer_resident((D_MODEL, D_MODEL), layer),
            _layer_resident((1, D_MODEL), layer),
            _layer_resident((D_MODEL, d_ff), layer),
            _layer_resident((D_MODEL, d_ff), layer),
            _layer_resident((d_ff, D_MODEL), layer),
        ],
        out_specs=pl.BlockSpec((tm, D_MODEL), row),
        out_shape=jax.ShapeDtypeStruct((t_rows, D_MODEL), F32),
        compiler_params=_params(1),
        name="merge_ffn",
    )(x2d, o_a, o_r, prm["w_out"], prm["norm_ffn"], prm["w_gate"], prm["w_up"], prm["w_down"])


def _layer(x, prm, layer, bias, tail_rows, cache_k=None, cache_v=None, state0=None):
    b, seq, _ = x.shape
    x2d = x.reshape(b * seq, D_MODEL)
    qa, ka, va, qr, kr, vr, gr, k_tail, v_tail = _proj(x2d, prm, layer, seq, tail_rows)
    r3 = lambda t: t.reshape(b, seq, SEG)
    o_a = _attention(r3(qa), r3(ka), r3(va), bias, layer, cache_k, cache_v)
    o_r, s_new = _retention(r3(qr), r3(kr), r3(vr), r3(gr), prm["ret_norm"], layer, state0)
    y = _merge_ffn(x2d, o_a.reshape(b * seq, WIDTH_A), o_r.reshape(b * seq, WIDTH_R), prm, layer)
    if k_tail.ndim == 3:
        heads = lambda t: t.reshape(b, HEADS_A, HEAD_DIM_A, tail_rows).transpose(0, 3, 1, 2)
    else:
        heads = lambda t: t.reshape(b, tail_rows, HEADS_A, HEAD_DIM_A)
    return y.reshape(b, seq, D_MODEL), heads(k_tail), heads(v_tail), s_new


def kernel(x_prompt, x_sample, cache_a_k, cache_a_v, state_ret, norm_mix, w_in, q_norm, k_norm,
           rel_table, ret_norm, w_out, norm_ffn, w_gate, w_up, w_down):
    depth = w_in.shape[0]
    bp, sq_p, _ = x_prompt.shape
    bs, sq_s, _ = x_sample.shape
    assert cache_a_k.shape[2] == BAND_ROWS
    rows_p = min(BAND_ROWS, sq_p)

    vec = lambda t: t.reshape(depth, 1, -1)
    prm = dict(
        norm_mix=vec(norm_mix), norm_ffn=vec(norm_ffn), ret_norm=vec(ret_norm),
        q_norm=vec(jnp.tile(q_norm, (1, HEADS_A))), k_norm=vec(jnp.tile(k_norm, (1, HEADS_A))),
        w_in=w_in.astype(BF16), w_out=w_out.astype(BF16),
        w_gate=w_gate.astype(BF16), w_up=w_up.astype(BF16), w_down=w_down.astype(BF16))
    bias_p = _band_bias(rel_table, ATTN_CHUNKS_PER_TILE)
    bias_s = _band_bias(rel_table, 1)
    feature_major = lambda c: c.transpose(0, 1, 3, 4, 2).reshape(depth, bs, WIDTH_A, BAND_ROWS)
    cache_k, cache_v = feature_major(cache_a_k), feature_major(cache_a_v)

    xp, xs = x_prompt, x_sample
    kp, vp, sp, ks, vs, ss = [], [], [], [], [], []
    for l in range(depth):
        xp, k_t, v_t, s_n = _layer(xp, prm, l, bias_p, rows_p)
        kp.append(k_t), vp.append(v_t), sp.append(s_n)
        xs, k_t, v_t, s_n = _layer(xs, prm, l, bias_s, sq_s, cache_k, cache_v, state_ret)
        ks.append(k_t), vs.append(v_t), ss.append(s_n)
    return (xp, xs, jnp.stack(kp), jnp.stack(vp), jnp.stack(sp),
            jnp.stack(ks), jnp.stack(vs), jnp.stack(ss))
```

```python
import functools

import jax
import jax.numpy as jnp
import numpy as np
from jax import lax
from jax.experimental import pallas as pl
from jax.experimental.pallas import tpu as pltpu

F32 = jnp.float32
BF16 = jnp.bfloat16

D_MODEL = 1024
CHUNK = 64
N_PREV_CHUNKS = 8
BAND_ROWS = N_PREV_CHUNKS * CHUNK
HEADS_A = 8
HEAD_DIM_A = 64
WIDTH_A = HEADS_A * HEAD_DIM_A
HEADS_R = 4
HEAD_DIM_R = 128
WIDTH_R = HEADS_R * HEAD_DIM_R
N_SEG = 7
SEG = 512
REL_CLIP = 2 * CHUNK
ROPE_BASE = 10000.0
RMS_EPS = 1e-6
GN_EPS = 1e-5
NEG_INF = -1e30
LOG2_E = 1.4426950408889634

VMEM_LIMIT_BYTES = 56 * 1024 * 1024
LANES = 128
MXU_DIM = 256


def _resident(shape):
    nd = len(shape)
    return pl.BlockSpec(shape, lambda *_: (0,) * nd, pipeline_mode=pl.Buffered(1))


def _layer_resident(shape, layer):
    nd = len(shape)
    return pl.BlockSpec((None,) + tuple(shape), lambda *_: (layer,) + (0,) * nd,
                        pipeline_mode=pl.Buffered(1))


def _params(n_axes):
    return pltpu.CompilerParams(dimension_semantics=("arbitrary",) * n_axes,
                                vmem_limit_bytes=VMEM_LIMIT_BYTES)


def _retention_block(q, k, v, g, s_prev, dec_ref, qd_ref, kd_ref, sd_ref, rn_ref):
    outs, s_new = [], []
    for hd in range(HEADS_R):
        sl = slice(hd * HEAD_DIM_R, (hd + 1) * HEAD_DIM_R)
        qh, kh, vh = q[:, sl], k[:, sl], v[:, sl]
        scores = lax.dot_general(qh, kh.astype(BF16), NT_DIMS, preferred_element_type=F32) * dec_ref[hd]
        intra = jnp.dot(scores.astype(BF16), vh, preferred_element_type=F32)
        cross = jnp.dot(qh, s_prev[hd].astype(BF16), preferred_element_type=F32) * qd_ref[:, sl]
        k_dec = (kh * kd_ref[:, sl]).astype(BF16)
        s_new.append(sd_ref[hd] * s_prev[hd] + lax.dot_general(
            k_dec, vh, (((0,), (0,)), ((), ())), preferred_element_type=F32))
        o = intra + cross
        mu = jnp.mean(o, axis=-1, keepdims=True)
        d = o - mu
        var = jnp.mean(d * d, axis=-1, keepdims=True)
        outs.append(d * lax.rsqrt(var + GN_EPS))
    o_norm = jnp.concatenate(outs, axis=-1) * rn_ref[...]
    return (g * jax.nn.sigmoid(g)) * o_norm, s_new


def _rebase_state(s, rc_ref, rs_ref):
    half = HEAD_DIM_R // 2
    s1, s2 = s[:half], s[half:]
    return jnp.concatenate([s1 * rc_ref[...] - s2 * rs_ref[...],
                            s1 * rs_ref[...] + s2 * rc_ref[...]], axis=0)


def _proj_kernel(*refs, tail_period, tail_first, ret_rows, seq_tiles, has_state):
    (x_ref, g_ref, w_ref, qn_ref, kn_ref, cos_ref, sin_ref, hm_ref,
     dec_ref, qd_ref, kd_ref, sd_ref, rn_ref, rc_ref, rs_ref) = refs[:15]
    n_in = 16 if has_state else 15
    qa_ref, ka_ref, va_ref, or_ref, kt_ref, vt_ref, sout_ref = refs[n_in:n_in + 7]
    if not has_state:
        tile_in_seq = pl.program_id(0) % seq_tiles

        @pl.when(tile_in_seq == 0)
        def _():
            refs[-1][...] = jnp.zeros(refs[-1].shape, F32)
    x = x_ref[...]
    ms = jnp.mean(x * x, axis=-1, keepdims=True)
    h = ((x * lax.rsqrt(ms + RMS_EPS)) * g_ref[...]).astype(BF16)

    def seg(j):
        return jnp.dot(h, w_ref[:, j * SEG:(j + 1) * SEG], preferred_element_type=F32)

    def head_rms(z, gain):
        z2 = (z * z).astype(BF16)
        parts = [jnp.dot(z2[:, c * MXU_DIM:(c + 1) * MXU_DIM], hm_ref[...], preferred_element_type=F32)
                 for c in range(SEG // MXU_DIM)]
        msq = jnp.concatenate(parts, axis=-1)
        return (z * lax.rsqrt(msq + RMS_EPS)) * gain

    def rotary(z):
        outs = []
        for hd in range(HEADS_R):
            zh = z[:, hd * HEAD_DIM_R:(hd + 1) * HEAD_DIM_R]
            outs.append(zh * cos_ref[...] + pltpu.roll(zh, HEAD_DIM_R // 2, axis=1) * sin_ref[...])
        return jnp.concatenate(outs, axis=-1)

    mixer_a = {}

    def project_qa():
        qa_ref[...] = (head_rms(seg(0), qn_ref[...]) * (HEAD_DIM_A ** -0.5 * LOG2_E)).astype(BF16)

    def project_ka():
        mixer_a["k"] = head_rms(seg(1), kn_ref[...])
        ka_ref[...] = mixer_a["k"].astype(BF16)

    def project_va():
        mixer_a["v"] = seg(2)
        va_ref[...] = mixer_a["v"].astype(BF16)

    pending = [project_qa, project_ka, project_va]

    qr = rotary(seg(3)).astype(BF16)
    kr = rotary(seg(4)) * (HEAD_DIM_R ** -0.5)
    vr = seg(5).astype(BF16)
    gr = seg(6)
    tables = (dec_ref, qd_ref, kd_ref, sd_ref, rn_ref)
    n_blocks = x.shape[0] // ret_rows
    if has_state:
        s0_ref = refs[15]
    else:
        state = refs[-1]
        s_cur = [state[hd] for hd in range(HEADS_R)]
    for j in range(n_blocks):
        rows = slice(j * ret_rows, (j + 1) * ret_rows)
        if has_state:
            s_cur = [s0_ref[j, hd] for hd in range(HEADS_R)]
        o, s_cur = _retention_block(qr[rows], kr[rows], vr[rows], gr[rows], s_cur, *tables)
        or_ref[rows, :] = o.astype(BF16)
        if has_state:
            for hd in range(HEADS_R):
                sout_ref[j, hd] = _rebase_state(s_cur[hd], rc_ref, rs_ref)
        if pending:
            pending.pop(0)()
    for task in pending:
        task()
    if not has_state:
        for hd in range(HEADS_R):
            state[hd] = s_cur[hd]

        @pl.when(tile_in_seq == seq_tiles - 1)
        def _():
            for hd in range(HEADS_R):
                sout_ref[0, hd] = _rebase_state(state[hd], rc_ref, rs_ref)

    ka, va = mixer_a["k"], mixer_a["v"]

    def write_tails():
        if len(kt_ref.shape) == 3:
            kt_ref[0] = ka.T
            vt_ref[0] = va.T
            return
        for hd in range(HEADS_A):
            rows = pl.ds(hd, ka.shape[0], stride=HEADS_A)
            cols = slice(hd * HEAD_DIM_A, (hd + 1) * HEAD_DIM_A)
            kt_ref[rows, :] = ka[:, cols]
            vt_ref[rows, :] = va[:, cols]

    if tail_period == 1:
        write_tails()
    else:
        pl.when(pl.program_id(0) % tail_period >= tail_first)(write_tails)


RETENTION_BLOCK = 256


def _retention_tables(blk, seq):
    log_gamma = jnp.log(1.0 - 2.0 ** (-5.0 - jnp.arange(HEADS_R, dtype=F32)))
    idx = jnp.arange(blk, dtype=F32)
    diff = idx[:, None] - idx[None, :]
    decay = jnp.where(diff[None] >= 0,
                      jnp.exp(jnp.maximum(diff, 0.0)[None] * log_gamma[:, None, None]), 0.0)
    lanes = lambda t: jnp.repeat(t, HEAD_DIM_R, axis=-1)
    q_decay = lanes(jnp.exp((idx + 1.0)[:, None] * log_gamma[None, :]))
    k_decay = lanes(jnp.exp((blk - 1.0 - idx)[:, None] * log_gamma[None, :]))
    s_decay = jnp.broadcast_to(jnp.exp(blk * log_gamma)[:, None, None], (HEADS_R, 1, HEAD_DIM_R))
    half = HEAD_DIM_R // 2
    inv = ROPE_BASE ** (-jnp.arange(half, dtype=F32) / half)
    ang = jnp.asarray(-seq, dtype=F32) * inv
    reb_cos = jnp.broadcast_to(jnp.cos(ang)[:, None], (half, HEAD_DIM_R))
    reb_sin = jnp.broadcast_to(jnp.sin(ang)[:, None], (half, HEAD_DIM_R))
    return decay, q_decay, k_decay, s_decay, reb_cos, reb_sin


def _proj(x2d, prm, layer, seq, tail_rows, state0=None, tm=512):
    t_rows = x2d.shape[0]
    assert t_rows % tm == 0
    n_tiles = t_rows // tm
    n_seq = t_rows // seq
    has_state = state0 is not None
    if seq >= tm:
        assert seq % tm == 0 and tail_rows % tm == 0 and not has_state
        tail_period = seq // tm
        tail_first = (seq - tail_rows) // tm
        table_rows = seq
        ret_rows = min(RETENTION_BLOCK, tm)
    else:
        assert tm % seq == 0 and tail_rows == seq and has_state
        tail_period, tail_first = 1, 0
        table_rows = tm
        ret_rows = seq
    assert tm % ret_rows == 0
    n_tab = table_rows // tm
    seqs_per_tile = max(1, tm // seq)
    decay, q_decay, k_decay, s_decay, reb_cos, reb_sin = _retention_tables(ret_rows, seq)

    half = HEAD_DIM_R // 2
    inv = ROPE_BASE ** (-jnp.arange(half, dtype=F32) / half)
    pos = (jnp.arange(table_rows) % seq).astype(F32)
    ang = pos[:, None] * inv
    cos, sin = jnp.cos(ang), jnp.sin(ang)
    cos_t = jnp.concatenate([cos, cos], axis=-1)
    sin_t = jnp.concatenate([-sin, sin], axis=-1)

    lane = np.arange(MXU_DIM)
    head_mean = jnp.asarray((lane[:, None] // HEAD_DIM_A == lane[None, :] // HEAD_DIM_A)
                            .astype(np.float32) / HEAD_DIM_A, dtype=BF16)

    row = lambda i: (i, 0)
    out_bf = jax.ShapeDtypeStruct((t_rows, SEG), BF16)
    state_shape = jax.ShapeDtypeStruct((n_seq, HEADS_R, HEAD_DIM_R, HEAD_DIM_R), F32)
    state_dims = (seqs_per_tile, HEADS_R, HEAD_DIM_R, HEAD_DIM_R)
    state_blk = pl.BlockSpec(state_dims, lambda i: (i // tail_period, 0, 0, 0))
    if seq >= tm:
        tail_shape = jax.ShapeDtypeStruct((t_rows // seq, SEG, tail_rows), F32)
        tail_blk = pl.BlockSpec((1, SEG, tm), lambda i: (
            i // tail_period, 0, jnp.maximum(i % tail_period - tail_first, 0)))
    else:
        tail_shape = jax.ShapeDtypeStruct((t_rows * HEADS_A, HEAD_DIM_A), F32)
        tail_blk = pl.BlockSpec((tm * HEADS_A, HEAD_DIM_A), row)
    blk = pl.BlockSpec((tm, SEG), row)
    in_specs = [
        pl.BlockSpec((tm, D_MODEL), row),
        _layer_resident((1, D_MODEL), layer),
        _layer_resident((D_MODEL, N_SEG * SEG), layer),
        _layer_resident((1, SEG), layer),
        _layer_resident((1, SEG), layer),
        pl.BlockSpec((tm, HEAD_DIM_R), lambda i: (i % n_tab, 0)),
        pl.BlockSpec((tm, HEAD_DIM_R), lambda i: (i % n_tab, 0)),
        _resident((MXU_DIM, MXU_DIM)),
        _resident(decay.shape), _resident(q_decay.shape), _resident(k_decay.shape),
        _resident(s_decay.shape), _layer_resident((1, WIDTH_R), layer),
        _resident(reb_cos.shape), _resident(reb_sin.shape),
    ]
    args = [x2d, prm["norm_mix"], prm["w_in"], prm["q_norm"], prm["k_norm"], cos_t, sin_t, head_mean,
            decay, q_decay, k_decay, s_decay, prm["ret_norm"], reb_cos, reb_sin]
    scratch = []
    if has_state:
        in_specs.append(pl.BlockSpec((None,) + state_dims, lambda i: (layer, i, 0, 0, 0)))
        args.append(state0)
    else:
        scratch.append(pltpu.VMEM((HEADS_R, HEAD_DIM_R, HEAD_DIM_R), F32))
    return pl.pallas_call(
        functools.partial(_proj_kernel, tail_period=tail_period, tail_first=tail_first,
                          ret_rows=ret_rows, seq_tiles=tail_period, has_state=has_state),
        grid=(n_tiles,),
        in_specs=in_specs,
        out_specs=[blk, blk, blk, blk, tail_blk, tail_blk, state_blk],
        out_shape=[out_bf, out_bf, out_bf, out_bf, tail_shape, tail_shape, state_shape],
        scratch_shapes=scratch,
        compiler_params=_params(1),
        name="proj",
    )(*args)


NT_DIMS = (((1,), (1,)), ((), ()))


def _pair_scores(tile, pair, bias_ref):
    q_pair = tile["q"](pair)
    tq = q_pair.shape[0]
    lane = lax.broadcasted_iota(jnp.int32, (tq, LANES), 1)
    zero = jnp.zeros_like(q_pair)
    qz = jnp.concatenate([jnp.where(lane < HEAD_DIM_A, q_pair, zero),
                          jnp.where(lane >= HEAD_DIM_A, q_pair, zero)], axis=0)
    return tile["qk"](pair, qz) + bias_ref[pair, :, tile["lo"]:]


def _pair_output(tile, pair, s):
    tq = s.shape[0] // 2
    m = jnp.max(s, axis=-1, keepdims=True)
    p = jnp.exp2(s - m)
    denom = jnp.sum(p, axis=-1, keepdims=True)
    r = tile["pv"](pair, p.astype(BF16)) / denom
    lane = lax.broadcasted_iota(jnp.int32, (tq, LANES), 1)
    return jnp.where(lane < HEAD_DIM_A, r[:tq], r[tq:])


def _run_attention_tiles(tiles, bias_ref):
    n_pairs = HEADS_A // 2
    units = [(tile, pair) for tile in tiles for pair in range(n_pairs)]
    s_next = _pair_scores(*units[0], bias_ref)
    outs = []
    for idx, (tile, pair) in enumerate(units):
        s_cur = s_next
        if idx + 1 < len(units):
            s_next = _pair_scores(*units[idx + 1], bias_ref)
        outs.append(_pair_output(tile, pair, s_cur))
        if pair == n_pairs - 1:
            tile["store"](jnp.concatenate(outs, axis=-1).astype(BF16))
            outs = []


def _attn_prompt_kernel(q_ref, k_ref, v_ref, bias_ref, o_ref, *, tq, tps, n_steps):
    nk = bias_ref.shape[-1]
    n_special = BAND_ROWS // tq
    step = pl.program_id(1)
    lanes = lambda pair: pl.ds(pair * LANES, LANES)

    def make_tile(i, row0, nkv):
        rows = pl.ds(row0, nkv)
        qrows = pl.ds(i * tq, tq)

        def store(o):
            o_ref[0, qrows, :] = o

        def qk(pair, qz):
            return lax.dot_general(qz, k_ref[0, rows, lanes(pair)], NT_DIMS, preferred_element_type=F32)

        def pv(pair, p):
            return jnp.dot(p, v_ref[0, rows, lanes(pair)], preferred_element_type=F32)
        return dict(q=lambda pair: q_ref[0, qrows, lanes(pair)], qk=qk, pv=pv, lo=nk - nkv, store=store)

    for j in range(min(n_special // tps, n_steps)):
        @pl.when(step == j)
        def _(j=j):
            _run_attention_tiles(
                [make_tile(i, 0, (j * tps + i + 1) * tq) for i in range(tps)], bias_ref)

    if n_steps > n_special // tps:
        @pl.when(step >= n_special // tps)
        def _():
            tiles = []
            for i in range(tps):
                row0 = pl.multiple_of((step * tps + i) * tq - BAND_ROWS, tq)
                tiles.append(make_tile(i, row0, nk))
            _run_attention_tiles(tiles, bias_ref)


def _attn_cache_kernel(q_ref, k_ref, v_ref, bias_ref, ck_ref, cv_ref, o_ref, *, bb):
    lanes = lambda pair: pl.ds(pair * LANES, LANES)

    def make_tile(bi):
        past = lambda ref, pair: ref[bi, lanes(pair), :].astype(BF16)

        def store(o):
            o_ref[bi] = o

        def qk(pair, qz):
            s_past = jnp.dot(qz, past(ck_ref, pair), preferred_element_type=F32)
            s_new = lax.dot_general(qz, k_ref[bi, :, lanes(pair)], NT_DIMS, preferred_element_type=F32)
            return jnp.concatenate([s_past, s_new], axis=-1)

        def pv(pair, p):
            return (lax.dot_general(p[:, :BAND_ROWS], past(cv_ref, pair), NT_DIMS, preferred_element_type=F32)
                    + jnp.dot(p[:, BAND_ROWS:], v_ref[bi, :, lanes(pair)], preferred_element_type=F32))
        return dict(q=lambda pair: q_ref[bi, :, lanes(pair)], qk=qk, pv=pv, lo=0, store=store)

    _run_attention_tiles([make_tile(bi) for bi in range(bb)], bias_ref)


def _band_bias(rel_table, chunks_per_tile):
    n_band = (N_PREV_CHUNKS + 1) * CHUNK
    table = rel_table.astype(F32)
    lead = table.shape[:-1]
    u_min = BAND_ROWS + REL_CLIP - (n_band - 1)
    n_far = CHUNK - 1 + BAND_ROWS - REL_CLIP
    g = jnp.concatenate([table[..., u_min:], jnp.broadcast_to(table[..., -1:], lead + (n_far,))], axis=-1)
    r = jnp.concatenate([g[..., ::-1], jnp.zeros(lead + (1,), F32)], axis=-1)
    period = r.shape[-1]
    tiled = jnp.tile(r, (1,) * len(lead) + (CHUNK + 1,))[..., :CHUNK * (period + 1)]
    hankel = tiled.reshape(lead + (CHUNK, period + 1))[..., :n_band]
    base = hankel[..., ::-1, :]
    pad_lead = ((0, 0),) * (len(lead) + 1)
    rows = [jnp.pad(base, pad_lead + ((i * CHUNK, (chunks_per_tile - 1 - i) * CHUNK),),
                    constant_values=NEG_INF) for i in range(chunks_per_tile)]
    bias = jnp.concatenate(rows, axis=-2) * LOG2_E
    return bias.reshape(lead[:-1] + (HEADS_A // 2, 2 * chunks_per_tile * CHUNK, bias.shape[-1]))


ATTN_CHUNKS_PER_TILE = 2
ATTN_TILES_PER_STEP = 2
ATTN_CACHE_BATCH = 4


def _attention(qa, ka, va, bias, layer, cache_k=None, cache_v=None):
    b, seq, _ = qa.shape
    out_shape = jax.ShapeDtypeStruct((b, seq, WIDTH_A), BF16)
    bias_spec = _layer_resident(bias.shape[1:], layer)
    if cache_k is None:
        tps = ATTN_TILES_PER_STEP
        tq = ATTN_CHUNKS_PER_TILE * CHUNK
        assert seq % (tq * tps) == 0 and (BAND_ROWS // tq) % tps == 0
        n_steps = seq // (tq * tps)
        tile = pl.BlockSpec((1, tq * tps, WIDTH_A), lambda i, t: (i, t, 0))
        whole = pl.BlockSpec((1, seq, WIDTH_A), lambda i, t: (i, 0, 0))
        return pl.pallas_call(
            functools.partial(_attn_prompt_kernel, tq=tq, tps=tps, n_steps=n_steps),
            grid=(b, n_steps),
            in_specs=[tile, whole, whole, bias_spec],
            out_specs=tile,
            out_shape=out_shape,
            compiler_params=_params(2),
            name="band_attention",
        )(qa, ka, va, bias)
    assert seq == CHUNK
    bb = ATTN_CACHE_BATCH
    assert b % bb == 0
    new = pl.BlockSpec((bb, seq, WIDTH_A), lambda i: (i, 0, 0))
    cache = pl.BlockSpec((None, bb, WIDTH_A, BAND_ROWS), lambda i: (layer, i, 0, 0))
    return pl.pallas_call(
        functools.partial(_attn_cache_kernel, bb=bb),
        grid=(b // bb,),
        in_specs=[new, new, new, bias_spec, cache, cache],
        out_specs=new,
        out_shape=out_shape,
        compiler_params=_params(1),
        name="band_attention_cache",
    )(qa, ka, va, bias, cache_k, cache_v)


def _ff_chunks(d_ff):
    n_tiles = d_ff // MXU_DIM
    first = (n_tiles + 1) // 2 * MXU_DIM
    return [c for c in (first, d_ff - first) if c > 0] if n_tiles > 1 else [d_ff]


def _merge_ffn_kernel(x_ref, oa_ref, or_ref, wo_ref, g_ref, wg_ref, wu_ref, wd_ref, y_ref):
    mix = jnp.concatenate([oa_ref[...], or_ref[...]], axis=-1)
    x = x_ref[...] + jnp.dot(mix, wo_ref[...], preferred_element_type=F32)
    ms = jnp.mean(x * x, axis=-1, keepdims=True)
    h = ((x * lax.rsqrt(ms + RMS_EPS)) * g_ref[...]).astype(BF16)
    y = x
    lo = 0
    for width in _ff_chunks(wg_ref.shape[1]):
        sl = slice(lo, lo + width)
        gate = jnp.dot(h, wg_ref[:, sl], preferred_element_type=F32)
        up = jnp.dot(h, wu_ref[:, sl], preferred_element_type=F32)
        act = ((gate * jax.nn.sigmoid(gate)) * up).astype(BF16)
        y = y + jnp.dot(act, wd_ref[sl, :], preferred_element_type=F32)
        lo += width
    y_ref[...] = y


def _merge_ffn(x2d, o_a, o_r, prm, layer, tm=512):
    t_rows = x2d.shape[0]
    d_ff = prm["w_gate"].shape[-1]
    assert t_rows % tm == 0
    row = lambda i: (i, 0)
    return pl.pallas_call(
        _merge_ffn_kernel,
        grid=(t_rows // tm,),
        in_specs=[
            pl.BlockSpec((tm, D_MODEL), row),
            pl.BlockSpec((tm, WIDTH_A), row),
            pl.BlockSpec((tm, WIDTH_R), row),
            _layer_resident((D_MODEL, D_MODEL), layer),
            _layer_resident((1, D_MODEL), layer),
            _layer_resident((D_MODEL, d_ff), layer),
            _layer_resident((D_MODEL, d_ff), layer),
            _layer_resident((d_ff, D_MODEL), layer),
        ],
        out_specs=pl.BlockSpec((tm, D_MODEL), row),
        out_shape=jax.ShapeDtypeStruct((t_rows, D_MODEL), F32),
        compiler_params=_params(1),
        name="merge_ffn",
    )(x2d, o_a, o_r, prm["w_out"], prm["norm_ffn"], prm["w_gate"], prm["w_up"], prm["w_down"])


def _layer(x, prm, layer, bias, tail_rows, cache_k=None, cache_v=None, state0=None):
    b, seq, _ = x.shape
    x2d = x.reshape(b * seq, D_MODEL)
    qa, ka, va, o_r, k_tail, v_tail, s_new = _proj(x2d, prm, layer, seq, tail_rows, state0)
    r3 = lambda t: t.reshape(b, seq, SEG)
    o_a = _attention(r3(qa), r3(ka), r3(va), bias, layer, cache_k, cache_v)
    y = _merge_ffn(x2d, o_a.reshape(b * seq, WIDTH_A), o_r, prm, layer)
    if k_tail.ndim == 3:
        heads = lambda t: t.reshape(b, HEADS_A, HEAD_DIM_A, tail_rows).transpose(0, 3, 1, 2)
    else:
        heads = lambda t: t.reshape(b, tail_rows, HEADS_A, HEAD_DIM_A)
    return y.reshape(b, seq, D_MODEL), heads(k_tail), heads(v_tail), s_new


def kernel(x_prompt, x_sample, cache_a_k, cache_a_v, state_ret, norm_mix, w_in, q_norm, k_norm,
           rel_table, ret_norm, w_out, norm_ffn, w_gate, w_up, w_down):
    depth = w_in.shape[0]
    bp, sq_p, _ = x_prompt.shape
    bs, sq_s, _ = x_sample.shape
    assert cache_a_k.shape[2] == BAND_ROWS
    rows_p = min(BAND_ROWS, sq_p)

    vec = lambda t: t.reshape(depth, 1, -1)
    prm = dict(
        norm_mix=vec(norm_mix), norm_ffn=vec(norm_ffn), ret_norm=vec(ret_norm),
        q_norm=vec(jnp.tile(q_norm, (1, HEADS_A))), k_norm=vec(jnp.tile(k_norm, (1, HEADS_A))),
        w_in=w_in.astype(BF16), w_out=w_out.astype(BF16),
        w_gate=w_gate.astype(BF16), w_up=w_up.astype(BF16), w_down=w_down.astype(BF16))
    bias_p = _band_bias(rel_table, ATTN_CHUNKS_PER_TILE)
    bias_s = _band_bias(rel_table, 1)
    feature_major = lambda c: c.transpose(0, 1, 3, 4, 2).reshape(depth, bs, WIDTH_A, BAND_ROWS)
    cache_k, cache_v = feature_major(cache_a_k), feature_major(cache_a_v)

    xp, xs = x_prompt, x_sample
    kp, vp, sp, ks, vs, ss = [], [], [], [], [], []
    for l in range(depth):
        xp, k_t, v_t, s_n = _layer(xp, prm, l, bias_p, rows_p)
        kp.append(k_t), vp.append(v_t), sp.append(s_n)
        xs, k_t, v_t, s_n = _layer(xs, prm, l, bias_s, sq_s, cache_k, cache_v, state_ret)
        ks.append(k_t), vs.append(v_t), ss.append(s_n)
    return (xp, xs, jnp.stack(kp), jnp.stack(vp), jnp.stack(sp),
            jnp.stack(ks), jnp.stack(vs), jnp.stack(ss))
```

```python
import functools

import jax
import jax.numpy as jnp
import numpy as np
from jax import lax
from jax.experimental import pallas as pl
from jax.experimental.pallas import tpu as pltpu

F32 = jnp.float32
BF16 = jnp.bfloat16

D_MODEL = 1024
CHUNK = 64
N_PREV_CHUNKS = 8
BAND_ROWS = N_PREV_CHUNKS * CHUNK
HEADS_A = 8
HEAD_DIM_A = 64
WIDTH_A = HEADS_A * HEAD_DIM_A
HEADS_R = 4
HEAD_DIM_R = 128
WIDTH_R = HEADS_R * HEAD_DIM_R
N_SEG = 7
SEG = 512
REL_CLIP = 2 * CHUNK
ROPE_BASE = 10000.0
RMS_EPS = 1e-6
GN_EPS = 1e-5
NEG_INF = -1e30
LOG2_E = 1.4426950408889634

VMEM_LIMIT_BYTES = 56 * 1024 * 1024
LANES = 128
MXU_DIM = 256


def _resident(shape):
    nd = len(shape)
    return pl.BlockSpec(shape, lambda *_: (0,) * nd, pipeline_mode=pl.Buffered(1))


def _layer_resident(shape, layer):
    nd = len(shape)
    return pl.BlockSpec((None,) + tuple(shape), lambda *_: (layer,) + (0,) * nd,
                        pipeline_mode=pl.Buffered(1))


def _params(n_axes):
    return pltpu.CompilerParams(dimension_semantics=("arbitrary",) * n_axes,
                                vmem_limit_bytes=VMEM_LIMIT_BYTES)


def _retention_block(q, k, v, g, s_prev, dec_ref, qd_ref, kd_ref, sd_ref, rn_ref):
    outs, s_new = [], []
    for hd in range(HEADS_R):
        sl = slice(hd * HEAD_DIM_R, (hd + 1) * HEAD_DIM_R)
        qh, kh, vh = q[:, sl], k[:, sl], v[:, sl]
        scores = lax.dot_general(qh, kh.astype(BF16), NT_DIMS, preferred_element_type=F32) * dec_ref[hd]
        intra = jnp.dot(scores.astype(BF16), vh, preferred_element_type=F32)
        cross = jnp.dot(qh, s_prev[hd].astype(BF16), preferred_element_type=F32) * qd_ref[:, sl]
        k_dec = (kh * kd_ref[:, sl]).astype(BF16)
        s_new.append(sd_ref[hd] * s_prev[hd] + lax.dot_general(
            k_dec, vh, (((0,), (0,)), ((), ())), preferred_element_type=F32))
        o = intra + cross
        mu = jnp.mean(o, axis=-1, keepdims=True)
        d = o - mu
        var = jnp.mean(d * d, axis=-1, keepdims=True)
        outs.append(d * lax.rsqrt(var + GN_EPS))
    o_norm = jnp.concatenate(outs, axis=-1) * rn_ref[...]
    return (g * jax.nn.sigmoid(g)) * o_norm, s_new


def _rebase_state(s, rc_ref, rs_ref):
    half = HEAD_DIM_R // 2
    s1, s2 = s[:half], s[half:]
    return jnp.concatenate([s1 * rc_ref[...] - s2 * rs_ref[...],
                            s1 * rs_ref[...] + s2 * rc_ref[...]], axis=0)


def _proj_kernel(*refs, tail_period, tail_first, ret_rows, seq_tiles, has_state):
    (x_ref, g_ref, w_ref, qn_ref, kn_ref, cos_ref, sin_ref, hm_ref,
     dec_ref, qd_ref, kd_ref, sd_ref, rn_ref, rc_ref, rs_ref) = refs[:15]
    n_in = 16 if has_state else 15
    qa_ref, ka_ref, va_ref, or_ref, kt_ref, vt_ref, sout_ref = refs[n_in:n_in + 7]
    if not has_state:
        tile_in_seq = pl.program_id(0) % seq_tiles

        @pl.when(tile_in_seq == 0)
        def _():
            refs[-1][...] = jnp.zeros(refs[-1].shape, F32)
    x = x_ref[...]
    ms = jnp.mean(x * x, axis=-1, keepdims=True)
    h = ((x * lax.rsqrt(ms + RMS_EPS)) * g_ref[...]).astype(BF16)

    def seg(j):
        return jnp.dot(h, w_ref[:, j * SEG:(j + 1) * SEG], preferred_element_type=F32)

    def head_rms(z, gain):
        z2 = (z * z).astype(BF16)
        parts = [jnp.dot(z2[:, c * MXU_DIM:(c + 1) * MXU_DIM], hm_ref[...], preferred_element_type=F32)
                 for c in range(SEG // MXU_DIM)]
        msq = jnp.concatenate(parts, axis=-1)
        return (z * lax.rsqrt(msq + RMS_EPS)) * gain

    def rotary(z):
        outs = []
        for hd in range(HEADS_R):
            zh = z[:, hd * HEAD_DIM_R:(hd + 1) * HEAD_DIM_R]
            outs.append(zh * cos_ref[...] + pltpu.roll(zh, HEAD_DIM_R // 2, axis=1) * sin_ref[...])
        return jnp.concatenate(outs, axis=-1)

    mixer_a = {}

    def project_qa():
        qa_ref[...] = (head_rms(seg(0), qn_ref[...]) * (HEAD_DIM_A ** -0.5 * LOG2_E)).astype(BF16)

    def project_ka():
        mixer_a["k"] = head_rms(seg(1), kn_ref[...])
        ka_ref[...] = mixer_a["k"].astype(BF16)

    def project_va():
        mixer_a["v"] = seg(2)
        va_ref[...] = mixer_a["v"].astype(BF16)

    pending = [project_qa, project_ka, project_va]

    qr = rotary(seg(3)).astype(BF16)
    kr = rotary(seg(4)) * (HEAD_DIM_R ** -0.5)
    vr = seg(5).astype(BF16)
    gr = seg(6)
    tables = (dec_ref, qd_ref, kd_ref, sd_ref, rn_ref)
    n_blocks = x.shape[0] // ret_rows
    if has_state:
        s0_ref = refs[15]
    else:
        state = refs[-1]
        s_cur = [state[hd] for hd in range(HEADS_R)]
    for j in range(n_blocks):
        rows = slice(j * ret_rows, (j + 1) * ret_rows)
        if has_state:
            s_cur = [s0_ref[j, hd] for hd in range(HEADS_R)]
        o, s_cur = _retention_block(qr[rows], kr[rows], vr[rows], gr[rows], s_cur, *tables)
        or_ref[rows, :] = o.astype(BF16)
        if has_state:
            for hd in range(HEADS_R):
                sout_ref[j, hd] = _rebase_state(s_cur[hd], rc_ref, rs_ref)
        if pending:
            pending.pop(0)()
    for task in pending:
        task()
    if not has_state:
        for hd in range(HEADS_R):
            state[hd] = s_cur[hd]

        @pl.when(tile_in_seq == seq_tiles - 1)
        def _():
            for hd in range(HEADS_R):
                sout_ref[0, hd] = _rebase_state(state[hd], rc_ref, rs_ref)

    ka, va = mixer_a["k"], mixer_a["v"]

    def write_tails():
        if len(kt_ref.shape) == 3:
            kt_ref[0] = ka.T
            vt_ref[0] = va.T
            return
        for hd in range(HEADS_A):
            rows = pl.ds(hd, ka.shape[0], stride=HEADS_A)
            cols = slice(hd * HEAD_DIM_A, (hd + 1) * HEAD_DIM_A)
            kt_ref[rows, :] = ka[:, cols]
            vt_ref[rows, :] = va[:, cols]

    if tail_period == 1:
        write_tails()
    else:
        pl.when(pl.program_id(0) % tail_period >= tail_first)(write_tails)


RETENTION_BLOCK = 256


def _retention_tables(blk, seq):
    log_gamma = jnp.log(1.0 - 2.0 ** (-5.0 - jnp.arange(HEADS_R, dtype=F32)))
    idx = jnp.arange(blk, dtype=F32)
    diff = idx[:, None] - idx[None, :]
    decay = jnp.where(diff[None] >= 0,
                      jnp.exp(jnp.maximum(diff, 0.0)[None] * log_gamma[:, None, None]), 0.0)
    lanes = lambda t: jnp.repeat(t, HEAD_DIM_R, axis=-1)
    q_decay = lanes(jnp.exp((idx + 1.0)[:, None] * log_gamma[None, :]))
    k_decay = lanes(jnp.exp((blk - 1.0 - idx)[:, None] * log_gamma[None, :]))
    s_decay = jnp.broadcast_to(jnp.exp(blk * log_gamma)[:, None, None], (HEADS_R, 1, HEAD_DIM_R))
    half = HEAD_DIM_R // 2
    inv = ROPE_BASE ** (-jnp.arange(half, dtype=F32) / half)
    ang = jnp.asarray(-seq, dtype=F32) * inv
    reb_cos = jnp.broadcast_to(jnp.cos(ang)[:, None], (half, HEAD_DIM_R))
    reb_sin = jnp.broadcast_to(jnp.sin(ang)[:, None], (half, HEAD_DIM_R))
    return decay, q_decay, k_decay, s_decay, reb_cos, reb_sin


def _proj(x2d, prm, layer, seq, tail_rows, state0=None, tm=512):
    t_rows = x2d.shape[0]
    assert t_rows % tm == 0
    n_tiles = t_rows // tm
    n_seq = t_rows // seq
    has_state = state0 is not None
    if seq >= tm:
        assert seq % tm == 0 and tail_rows % tm == 0 and not has_state
        tail_period = seq // tm
        tail_first = (seq - tail_rows) // tm
        table_rows = seq
        ret_rows = min(RETENTION_BLOCK, tm)
    else:
        assert tm % seq == 0 and tail_rows == seq and has_state
        tail_period, tail_first = 1, 0
        table_rows = tm
        ret_rows = seq
    assert tm % ret_rows == 0
    n_tab = table_rows // tm
    seqs_per_tile = max(1, tm // seq)
    decay, q_decay, k_decay, s_decay, reb_cos, reb_sin = _retention_tables(ret_rows, seq)

    half = HEAD_DIM_R // 2
    inv = ROPE_BASE ** (-jnp.arange(half, dtype=F32) / half)
    pos = (jnp.arange(table_rows) % seq).astype(F32)
    ang = pos[:, None] * inv
    cos, sin = jnp.cos(ang), jnp.sin(ang)
    cos_t = jnp.concatenate([cos, cos], axis=-1)
    sin_t = jnp.concatenate([-sin, sin], axis=-1)

    lane = np.arange(MXU_DIM)
    head_mean = jnp.asarray((lane[:, None] // HEAD_DIM_A == lane[None, :] // HEAD_DIM_A)
                            .astype(np.float32) / HEAD_DIM_A, dtype=BF16)

    row = lambda i: (i, 0)
    out_bf = jax.ShapeDtypeStruct((t_rows, SEG), BF16)
    state_shape = jax.ShapeDtypeStruct((n_seq, HEADS_R, HEAD_DIM_R, HEAD_DIM_R), F32)
    state_dims = (seqs_per_tile, HEADS_R, HEAD_DIM_R, HEAD_DIM_R)
    state_blk = pl.BlockSpec(state_dims, lambda i: (i // tail_period, 0, 0, 0))
    if seq >= tm:
        tail_shape = jax.ShapeDtypeStruct((t_rows // seq, SEG, tail_rows), F32)
        tail_blk = pl.BlockSpec((1, SEG, tm), lambda i: (
            i // tail_period, 0, jnp.maximum(i % tail_period - tail_first, 0)))
    else:
        tail_shape = jax.ShapeDtypeStruct((t_rows * HEADS_A, HEAD_DIM_A), F32)
        tail_blk = pl.BlockSpec((tm * HEADS_A, HEAD_DIM_A), row)
    blk = pl.BlockSpec((tm, SEG), row)
    in_specs = [
        pl.BlockSpec((tm, D_MODEL), row),
        _layer_resident((1, D_MODEL), layer),
        _layer_resident((D_MODEL, N_SEG * SEG), layer),
        _layer_resident((1, SEG), layer),
        _layer_resident((1, SEG), layer),
        pl.BlockSpec((tm, HEAD_DIM_R), lambda i: (i % n_tab, 0)),
        pl.BlockSpec((tm, HEAD_DIM_R), lambda i: (i % n_tab, 0)),
        _resident((MXU_DIM, MXU_DIM)),
        _resident(decay.shape), _resident(q_decay.shape), _resident(k_decay.shape),
        _resident(s_decay.shape), _layer_resident((1, WIDTH_R), layer),
        _resident(reb_cos.shape), _resident(reb_sin.shape),
    ]
    args = [x2d, prm["norm_mix"], prm["w_in"], prm["q_norm"], prm["k_norm"], cos_t, sin_t, head_mean,
            decay, q_decay, k_decay, s_decay, prm["ret_norm"], reb_cos, reb_sin]
    scratch = []
    if has_state:
        in_specs.append(pl.BlockSpec((None,) + state_dims, lambda i: (layer, i, 0, 0, 0)))
        args.append(state0)
    else:
        scratch.append(pltpu.VMEM((HEADS_R, HEAD_DIM_R, HEAD_DIM_R), F32))
    return pl.pallas_call(
        functools.partial(_proj_kernel, tail_period=tail_period, tail_first=tail_first,
                          ret_rows=ret_rows, seq_tiles=tail_period, has_state=has_state),
        grid=(n_tiles,),
        in_specs=in_specs,
        out_specs=[blk, blk, blk, blk, tail_blk, tail_blk, state_blk],
        out_shape=[out_bf, out_bf, out_bf, out_bf, tail_shape, tail_shape, state_shape],
        scratch_shapes=scratch,
        compiler_params=_params(1),
        name="proj",
    )(*args)


NT_DIMS = (((1,), (1,)), ((), ()))


def _pair_scores(tile, pair, bias_ref):
    q_pair = tile["q"](pair)
    tq = q_pair.shape[0]
    lane = lax.broadcasted_iota(jnp.int32, (tq, LANES), 1)
    zero = jnp.zeros_like(q_pair)
    qz = jnp.concatenate([jnp.where(lane < HEAD_DIM_A, q_pair, zero),
                          jnp.where(lane >= HEAD_DIM_A, q_pair, zero)], axis=0)
    s = tile["qk"](pair, qz) + bias_ref[pair, :, tile["lo"]:]
    return tile["mask"](s) if "mask" in tile else s


def _pair_output(tile, pair, s):
    tq = s.shape[0] // 2
    m = jnp.max(s, axis=-1, keepdims=True)
    p = jnp.exp2(s - m)
    denom = jnp.sum(p, axis=-1, keepdims=True)
    r = tile["pv"](pair, p.astype(BF16)) / denom
    lane = lax.broadcasted_iota(jnp.int32, (tq, LANES), 1)
    return jnp.where(lane < HEAD_DIM_A, r[:tq], r[tq:])


def _attention_units(tiles, bias_ref):
    n_pairs = HEADS_A // 2
    units = [(tile, pair) for tile in tiles for pair in range(n_pairs)]
    s_next = _pair_scores(*units[0], bias_ref)
    yield
    outs = []
    for idx, (tile, pair) in enumerate(units):
        s_cur = s_next
        if idx + 1 < len(units):
            s_next = _pair_scores(*units[idx + 1], bias_ref)
        outs.append(_pair_output(tile, pair, s_cur))
        if pair == n_pairs - 1:
            tile["store"](jnp.concatenate(outs, axis=-1).astype(BF16))
            outs = []
        yield


def _run_attention_tiles(tiles, bias_ref):
    for _ in _attention_units(tiles, bias_ref):
        pass


def _attn_prompt_kernel(q_ref, k_ref, v_ref, bias_ref, o_ref, *, tq, tps, n_steps):
    nk = bias_ref.shape[-1]
    n_special = BAND_ROWS // tq
    step = pl.program_id(1)
    lanes = lambda pair: pl.ds(pair * LANES, LANES)

    def make_tile(i, row0, nkv):
        rows = pl.ds(row0, nkv)
        qrows = pl.ds(i * tq, tq)

        def store(o):
            o_ref[0, qrows, :] = o

        def qk(pair, qz):
            return lax.dot_general(qz, k_ref[0, rows, lanes(pair)], NT_DIMS, preferred_element_type=F32)

        def pv(pair, p):
            return jnp.dot(p, v_ref[0, rows, lanes(pair)], preferred_element_type=F32)
        return dict(q=lambda pair: q_ref[0, qrows, lanes(pair)], qk=qk, pv=pv, lo=nk - nkv, store=store)

    for j in range(min(n_special // tps, n_steps)):
        @pl.when(step == j)
        def _(j=j):
            _run_attention_tiles(
                [make_tile(i, 0, (j * tps + i + 1) * tq) for i in range(tps)], bias_ref)

    if n_steps > n_special // tps:
        @pl.when(step >= n_special // tps)
        def _():
            tiles = []
            for i in range(tps):
                row0 = pl.multiple_of((step * tps + i) * tq - BAND_ROWS, tq)
                tiles.append(make_tile(i, row0, nk))
            _run_attention_tiles(tiles, bias_ref)


def _attn_cache_kernel(q_ref, k_ref, v_ref, bias_ref, ck_ref, cv_ref, o_ref, *, bb):
    lanes = lambda pair: pl.ds(pair * LANES, LANES)

    def make_tile(bi):
        past = lambda ref, pair: ref[bi, lanes(pair), :].astype(BF16)

        def store(o):
            o_ref[bi] = o

        def qk(pair, qz):
            s_past = jnp.dot(qz, past(ck_ref, pair), preferred_element_type=F32)
            s_new = lax.dot_general(qz, k_ref[bi, :, lanes(pair)], NT_DIMS, preferred_element_type=F32)
            return jnp.concatenate([s_past, s_new], axis=-1)

        def pv(pair, p):
            return (lax.dot_general(p[:, :BAND_ROWS], past(cv_ref, pair), NT_DIMS, preferred_element_type=F32)
                    + jnp.dot(p[:, BAND_ROWS:], v_ref[bi, :, lanes(pair)], preferred_element_type=F32))
        return dict(q=lambda pair: q_ref[bi, :, lanes(pair)], qk=qk, pv=pv, lo=0, store=store)

    _run_attention_tiles([make_tile(bi) for bi in range(bb)], bias_ref)


def _band_bias(rel_table, chunks_per_tile):
    n_band = (N_PREV_CHUNKS + 1) * CHUNK
    table = rel_table.astype(F32)
    lead = table.shape[:-1]
    u_min = BAND_ROWS + REL_CLIP - (n_band - 1)
    n_far = CHUNK - 1 + BAND_ROWS - REL_CLIP
    g = jnp.concatenate([table[..., u_min:], jnp.broadcast_to(table[..., -1:], lead + (n_far,))], axis=-1)
    r = jnp.concatenate([g[..., ::-1], jnp.zeros(lead + (1,), F32)], axis=-1)
    period = r.shape[-1]
    tiled = jnp.tile(r, (1,) * len(lead) + (CHUNK + 1,))[..., :CHUNK * (period + 1)]
    hankel = tiled.reshape(lead + (CHUNK, period + 1))[..., :n_band]
    base = hankel[..., ::-1, :]
    pad_lead = ((0, 0),) * (len(lead) + 1)
    rows = [jnp.pad(base, pad_lead + ((i * CHUNK, (chunks_per_tile - 1 - i) * CHUNK),),
                    constant_values=NEG_INF) for i in range(chunks_per_tile)]
    bias = jnp.concatenate(rows, axis=-2) * LOG2_E
    return bias.reshape(lead[:-1] + (HEADS_A // 2, 2 * chunks_per_tile * CHUNK, bias.shape[-1]))


ATTN_CHUNKS_PER_TILE = 2
ATTN_TILES_PER_STEP = 2
ATTN_CACHE_BATCH = 4


def _attention(qa, ka, va, bias, layer, cache_k=None, cache_v=None):
    b, seq, _ = qa.shape
    out_shape = jax.ShapeDtypeStruct((b, seq, WIDTH_A), BF16)
    bias_spec = _layer_resident(bias.shape[1:], layer)
    if cache_k is None:
        tps = ATTN_TILES_PER_STEP
        tq = ATTN_CHUNKS_PER_TILE * CHUNK
        assert seq % (tq * tps) == 0 and (BAND_ROWS // tq) % tps == 0
        n_steps = seq // (tq * tps)
        tile = pl.BlockSpec((1, tq * tps, WIDTH_A), lambda i, t: (i, t, 0))
        whole = pl.BlockSpec((1, seq, WIDTH_A), lambda i, t: (i, 0, 0))
        return pl.pallas_call(
            functools.partial(_attn_prompt_kernel, tq=tq, tps=tps, n_steps=n_steps),
            grid=(b, n_steps),
            in_specs=[tile, whole, whole, bias_spec],
            out_specs=tile,
            out_shape=out_shape,
            compiler_params=_params(2),
            name="band_attention",
        )(qa, ka, va, bias)
    assert seq == CHUNK
    bb = ATTN_CACHE_BATCH
    assert b % bb == 0
    new = pl.BlockSpec((bb, seq, WIDTH_A), lambda i: (i, 0, 0))
    cache = pl.BlockSpec((None, bb, WIDTH_A, BAND_ROWS), lambda i: (layer, i, 0, 0))
    return pl.pallas_call(
        functools.partial(_attn_cache_kernel, bb=bb),
        grid=(b // bb,),
        in_specs=[new, new, new, bias_spec, cache, cache],
        out_specs=new,
        out_shape=out_shape,
        compiler_params=_params(1),
        name="band_attention_cache",
    )(qa, ka, va, bias, cache_k, cache_v)


def _ff_chunks(d_ff):
    n_tiles = d_ff // MXU_DIM
    first = (n_tiles + 1) // 2 * MXU_DIM
    return [c for c in (first, d_ff - first) if c > 0] if n_tiles > 1 else [d_ff]


def _merge_rows(x, o_a, o_r, wo_ref, g_ref):
    mix = jnp.concatenate([o_a, o_r], axis=-1)
    x = x + jnp.dot(mix, wo_ref[...], preferred_element_type=F32)
    ms = jnp.mean(x * x, axis=-1, keepdims=True)
    return x, ((x * lax.rsqrt(ms + RMS_EPS)) * g_ref[...]).astype(BF16)


def _ffn_piece(h, y, wg_ref, wu_ref, wd_ref, lo, width):
    sl = slice(lo, lo + width)
    gate = jnp.dot(h, wg_ref[:, sl], preferred_element_type=F32)
    up = jnp.dot(h, wu_ref[:, sl], preferred_element_type=F32)
    act = ((gate * jax.nn.sigmoid(gate)) * up).astype(BF16)
    return y + jnp.dot(act, wd_ref[sl, :], preferred_element_type=F32)


def _merge_ffn_kernel(x_ref, oa_ref, or_ref, wo_ref, g_ref, wg_ref, wu_ref, wd_ref, y_ref):
    y, h = _merge_rows(x_ref[...], oa_ref[...], or_ref[...], wo_ref, g_ref)
    lo = 0
    for width in _ff_chunks(wg_ref.shape[1]):
        y = _ffn_piece(h, y, wg_ref, wu_ref, wd_ref, lo, width)
        lo += width
    y_ref[...] = y


def _attn_ffn_kernel(x_ref, or_ref, q0_ref, qn_ref, k_ref, v_ref, bias_ref,
                     wo_ref, g_ref, wg_ref, wu_ref, wd_ref, y_ref, oa_buf, kbuf, vbuf,
                     *, tq, seq_tiles, n_tiles):
    t = pl.program_id(0)
    tm = x_ref.shape[0]
    nk = bias_ref.shape[-1]
    q_tiles = tm // tq
    lanes = lambda pair: pl.ds(pair * LANES, LANES)
    nxt = jnp.minimum(t + 1, n_tiles - 1)

    def make_tile(q_ref, i, row0):
        rows = pl.ds(row0, nk)
        qrows = pl.ds(i * tq, tq)
        col = lax.broadcasted_iota(jnp.int32, (2 * tq, nk), 1)

        def store(o):
            oa_buf[qrows, :] = o

        def qk(pair, qz):
            return lax.dot_general(qz, kbuf[rows, lanes(pair)], NT_DIMS, preferred_element_type=F32)

        def pv(pair, p):
            return jnp.dot(p, vbuf[rows, lanes(pair)], preferred_element_type=F32)
        return dict(q=lambda pair: q_ref[qrows, lanes(pair)], qk=qk, pv=pv, lo=0, store=store,
                    mask=lambda s: jnp.where(col + row0 >= BAND_ROWS, s, NEG_INF))

    @pl.when(t == 0)
    def _():
        kbuf[0:BAND_ROWS, :] = jnp.zeros((BAND_ROWS, WIDTH_A), BF16)
        vbuf[0:BAND_ROWS, :] = jnp.zeros((BAND_ROWS, WIDTH_A), BF16)

    @pl.when((t == 0) | (nxt % seq_tiles == 0))
    def _():
        kbuf[BAND_ROWS:, :] = k_ref[0]
        vbuf[BAND_ROWS:, :] = v_ref[0]

    @pl.when(t == 0)
    def _():
        _run_attention_tiles([make_tile(q0_ref, i, i * tq) for i in range(q_tiles)], bias_ref)

    base = (nxt % seq_tiles) * tm
    units = _attention_units(
        [make_tile(qn_ref, i, pl.multiple_of(base + i * tq, tq)) for i in range(q_tiles)], bias_ref)
    n_units = q_tiles * (HEADS_A // 2)
    widths = [MXU_DIM] * (wg_ref.shape[1] // MXU_DIM)
    n_pieces = len(widths) + 1
    done = 0

    def advance(piece):
        nonlocal done
        target = (piece + 1) * n_units // n_pieces
        for _ in range(target - done):
            next(units, None)
        done = target

    next(units)
    y, h = _merge_rows(x_ref[...], oa_buf[...], or_ref[...], wo_ref, g_ref)
    advance(0)
    for c, width in enumerate(widths):
        y = _ffn_piece(h, y, wg_ref, wu_ref, wd_ref, c * width, width)
        advance(c + 1)
    y_ref[...] = y


def _merge_ffn(x2d, o_a, o_r, prm, layer, tm=512):
    t_rows = x2d.shape[0]
    d_ff = prm["w_gate"].shape[-1]
    assert t_rows % tm == 0
    row = lambda i: (i, 0)
    return pl.pallas_call(
        _merge_ffn_kernel,
        grid=(t_rows // tm,),
        in_specs=[
            pl.BlockSpec((tm, D_MODEL), row),
            pl.BlockSpec((tm, WIDTH_A), row),
            pl.BlockSpec((tm, WIDTH_R), row),
            _layer_resident((D_MODEL, D_MODEL), layer),
            _layer_resident((1, D_MODEL), layer),
            _layer_resident((D_MODEL, d_ff), layer),
            _layer_resident((D_MODEL, d_ff), layer),
            _layer_resident((d_ff, D_MODEL), layer),
        ],
        out_specs=pl.BlockSpec((tm, D_MODEL), row),
        out_shape=jax.ShapeDtypeStruct((t_rows, D_MODEL), F32),
        compiler_params=_params(1),
        name="merge_ffn",
    )(x2d, o_a, o_r, prm["w_out"], prm["norm_ffn"], prm["w_gate"], prm["w_up"], prm["w_down"])


def _attn_ffn(x2d, o_r, qa, ka, va, bias, prm, layer, seq, tm=512):
    t_rows = x2d.shape[0]
    d_ff = prm["w_gate"].shape[-1]
    tq = ATTN_CHUNKS_PER_TILE * CHUNK
    assert t_rows % tm == 0 and seq % tm == 0 and tm % tq == 0 and d_ff % MXU_DIM == 0
    n_tiles = t_rows // tm
    seq_tiles = seq // tm
    row = lambda i: (i, 0)
    nxt = lambda i: jnp.minimum(i + 1, n_tiles - 1)
    whole = pl.BlockSpec((1, seq, WIDTH_A), lambda i: (nxt(i) // seq_tiles, 0, 0),
                         pipeline_mode=pl.Buffered(1))
    return pl.pallas_call(
        functools.partial(_attn_ffn_kernel, tq=tq, seq_tiles=seq_tiles, n_tiles=n_tiles),
        grid=(n_tiles,),
        in_specs=[
            pl.BlockSpec((tm, D_MODEL), row),
            pl.BlockSpec((tm, WIDTH_R), row),
            pl.BlockSpec((tm, WIDTH_A), lambda i: (0, 0), pipeline_mode=pl.Buffered(1)),
            pl.BlockSpec((tm, WIDTH_A), lambda i: (nxt(i), 0)),
            whole, whole,
            _layer_resident(bias.shape[1:], layer),
            _layer_resident((D_MODEL, D_MODEL), layer),
            _layer_resident((1, D_MODEL), layer),
            _layer_resident((D_MODEL, d_ff), layer),
            _layer_resident((D_MODEL, d_ff), layer),
            _layer_resident((d_ff, D_MODEL), layer),
        ],
        out_specs=pl.BlockSpec((tm, D_MODEL), row),
        out_shape=jax.ShapeDtypeStruct((t_rows, D_MODEL), F32),
        scratch_shapes=[pltpu.VMEM((tm, WIDTH_A), BF16),
                        pltpu.VMEM((BAND_ROWS + seq, WIDTH_A), BF16),
                        pltpu.VMEM((BAND_ROWS + seq, WIDTH_A), BF16)],
        compiler_params=_params(1),
        name="attn_ffn",
    )(x2d, o_r, qa, qa, ka, va, bias, prm["w_out"], prm["norm_ffn"],
      prm["w_gate"], prm["w_up"], prm["w_down"])


def _layer(x, prm, layer, bias, tail_rows, cache_k=None, cache_v=None, state0=None):
    b, seq, _ = x.shape
    x2d = x.reshape(b * seq, D_MODEL)
    qa, ka, va, o_r, k_tail, v_tail, s_new = _proj(x2d, prm, layer, seq, tail_rows, state0)
    r3 = lambda t: t.reshape(b, seq, SEG)
    if cache_k is None:
        y = _attn_ffn(x2d, o_r, qa, r3(ka), r3(va), bias, prm, layer, seq)
    else:
        o_a = _attention(r3(qa), r3(ka), r3(va), bias, layer, cache_k, cache_v)
        y = _merge_ffn(x2d, o_a.reshape(b * seq, WIDTH_A), o_r, prm, layer)
    if k_tail.ndim == 3:
        heads = lambda t: t.reshape(b, HEADS_A, HEAD_DIM_A, tail_rows).transpose(0, 3, 1, 2)
    else:
        heads = lambda t: t.reshape(b, tail_rows, HEADS_A, HEAD_DIM_A)
    return y.reshape(b, seq, D_MODEL), heads(k_tail), heads(v_tail), s_new


def kernel(x_prompt, x_sample, cache_a_k, cache_a_v, state_ret, norm_mix, w_in, q_norm, k_norm,
           rel_table, ret_norm, w_out, norm_ffn, w_gate, w_up, w_down):
    depth = w_in.shape[0]
    bp, sq_p, _ = x_prompt.shape
    bs, sq_s, _ = x_sample.shape
    assert cache_a_k.shape[2] == BAND_ROWS
    rows_p = min(BAND_ROWS, sq_p)

    vec = lambda t: t.reshape(depth, 1, -1)
    prm = dict(
        norm_mix=vec(norm_mix), norm_ffn=vec(norm_ffn), ret_norm=vec(ret_norm),
        q_norm=vec(jnp.tile(q_norm, (1, HEADS_A))), k_norm=vec(jnp.tile(k_norm, (1, HEADS_A))),
        w_in=w_in.astype(BF16), w_out=w_out.astype(BF16),
        w_gate=w_gate.astype(BF16), w_up=w_up.astype(BF16), w_down=w_down.astype(BF16))
    bias_p = _band_bias(rel_table, ATTN_CHUNKS_PER_TILE)
    bias_s = _band_bias(rel_table, 1)
    feature_major = lambda c: c.transpose(0, 1, 3, 4, 2).reshape(depth, bs, WIDTH_A, BAND_ROWS)
    cache_k, cache_v = feature_major(cache_a_k), feature_major(cache_a_v)

    xp, xs = x_prompt, x_sample
    kp, vp, sp, ks, vs, ss = [], [], [], [], [], []
    for l in range(depth):
        xp, k_t, v_t, s_n = _layer(xp, prm, l, bias_p, rows_p)
        kp.append(k_t), vp.append(v_t), sp.append(s_n)
        xs, k_t, v_t, s_n = _layer(xs, prm, l, bias_s, sq_s, cache_k, cache_v, state_ret)
        ks.append(k_t), vs.append(v_t), ss.append(s_n)
    return (xp, xs, jnp.stack(kp), jnp.stack(vp), jnp.stack(sp),
            jnp.stack(ks), jnp.stack(vs), jnp.stack(ss))
```

```python
import functools

import jax
import jax.numpy as jnp
import numpy as np
from jax import lax
from jax.experimental import pallas as pl
from jax.experimental.pallas import tpu as pltpu

F32 = jnp.float32
BF16 = jnp.bfloat16

D_MODEL = 1024
CHUNK = 64
N_PREV_CHUNKS = 8
BAND_ROWS = N_PREV_CHUNKS * CHUNK
HEADS_A = 8
HEAD_DIM_A = 64
WIDTH_A = HEADS_A * HEAD_DIM_A
HEADS_R = 4
HEAD_DIM_R = 128
WIDTH_R = HEADS_R * HEAD_DIM_R
N_SEG = 7
SEG = 512
REL_CLIP = 2 * CHUNK
ROPE_BASE = 10000.0
RMS_EPS = 1e-6
GN_EPS = 1e-5
NEG_INF = -1e30
LOG2_E = 1.4426950408889634

VMEM_LIMIT_BYTES = 56 * 1024 * 1024
LANES = 128
MXU_DIM = 256


def _resident(shape):
    nd = len(shape)
    return pl.BlockSpec(shape, lambda *_: (0,) * nd, pipeline_mode=pl.Buffered(1))


def _layer_resident(shape, layer):
    nd = len(shape)
    return pl.BlockSpec((None,) + tuple(shape), lambda *_: (layer,) + (0,) * nd,
                        pipeline_mode=pl.Buffered(1))


def _params(n_axes):
    return pltpu.CompilerParams(dimension_semantics=("arbitrary",) * n_axes,
                                vmem_limit_bytes=VMEM_LIMIT_BYTES)


def _retention_block(q, k, v, g, s_prev, dec_ref, qd_ref, kd_ref, sd_ref, rn_ref):
    outs, s_new = [], []
    for hd in range(HEADS_R):
        sl = slice(hd * HEAD_DIM_R, (hd + 1) * HEAD_DIM_R)
        qh, kh, vh = q[:, sl], k[:, sl], v[:, sl]
        scores = lax.dot_general(qh, kh.astype(BF16), NT_DIMS, preferred_element_type=F32) * dec_ref[hd]
        intra = jnp.dot(scores.astype(BF16), vh, preferred_element_type=F32)
        cross = jnp.dot(qh, s_prev[hd].astype(BF16), preferred_element_type=F32) * qd_ref[:, sl]
        k_dec = (kh * kd_ref[:, sl]).astype(BF16)
        s_new.append(sd_ref[hd] * s_prev[hd] + lax.dot_general(
            k_dec, vh, (((0,), (0,)), ((), ())), preferred_element_type=F32))
        o = intra + cross
        mu = jnp.mean(o, axis=-1, keepdims=True)
        d = o - mu
        var = jnp.mean(d * d, axis=-1, keepdims=True)
        outs.append(d * lax.rsqrt(var + GN_EPS))
    o_norm = jnp.concatenate(outs, axis=-1) * rn_ref[...]
    return (g * jax.nn.sigmoid(g)) * o_norm, s_new


def _rebase_state(s, rc_ref, rs_ref):
    half = HEAD_DIM_R // 2
    s1, s2 = s[:half], s[half:]
    return jnp.concatenate([s1 * rc_ref[...] - s2 * rs_ref[...],
                            s1 * rs_ref[...] + s2 * rc_ref[...]], axis=0)


def _proj_kernel(*refs, slot, tail_period, tail_first, ret_rows, seq_tiles, has_state):
    (x_ref, g_ref, w_ref, qn_ref, kn_ref, cos_ref, sin_ref, hm_ref,
     dec_ref, qd_ref, kd_ref, sd_ref, rn_ref, rc_ref, rs_ref) = refs[:15]
    n_in = 15
    if has_state:
        s0_ref = refs[n_in]
        n_in += 1
    if slot > 0:
        kt_prev, vt_prev, sout_prev = refs[n_in:n_in + 3]
        n_in += 3
    qa_ref, ka_ref, va_ref, or_ref, kt_ref, vt_ref, sout_ref = refs[n_in:n_in + 7]
    if not has_state:
        tile_in_seq = pl.program_id(0) % seq_tiles

        @pl.when(tile_in_seq == 0)
        def _():
            refs[-1][...] = jnp.zeros(refs[-1].shape, F32)
    x = x_ref[...]
    ms = jnp.mean(x * x, axis=-1, keepdims=True)
    h = ((x * lax.rsqrt(ms + RMS_EPS)) * g_ref[...]).astype(BF16)

    def seg(j):
        return jnp.dot(h, w_ref[:, j * SEG:(j + 1) * SEG], preferred_element_type=F32)

    def head_rms(z, gain):
        z2 = (z * z).astype(BF16)
        parts = [jnp.dot(z2[:, c * MXU_DIM:(c + 1) * MXU_DIM], hm_ref[...], preferred_element_type=F32)
                 for c in range(SEG // MXU_DIM)]
        msq = jnp.concatenate(parts, axis=-1)
        return (z * lax.rsqrt(msq + RMS_EPS)) * gain

    def rotary(z):
        outs = []
        for hd in range(HEADS_R):
            zh = z[:, hd * HEAD_DIM_R:(hd + 1) * HEAD_DIM_R]
            outs.append(zh * cos_ref[...] + pltpu.roll(zh, HEAD_DIM_R // 2, axis=1) * sin_ref[...])
        return jnp.concatenate(outs, axis=-1)

    mixer_a = {}

    def project_qa():
        qa_ref[...] = (head_rms(seg(0), qn_ref[...]) * (HEAD_DIM_A ** -0.5 * LOG2_E)).astype(BF16)

    def project_ka():
        mixer_a["k"] = head_rms(seg(1), kn_ref[...])
        ka_ref[...] = mixer_a["k"].astype(BF16)

    def project_va():
        mixer_a["v"] = seg(2)
        va_ref[...] = mixer_a["v"].astype(BF16)

    pending = [project_qa, project_ka, project_va]

    qr = rotary(seg(3)).astype(BF16)
    kr = rotary(seg(4)) * (HEAD_DIM_R ** -0.5)
    vr = seg(5).astype(BF16)
    gr = seg(6)
    tables = (dec_ref, qd_ref, kd_ref, sd_ref, rn_ref)
    n_blocks = x.shape[0] // ret_rows
    if not has_state:
        state = refs[-1]
        s_cur = [state[hd] for hd in range(HEADS_R)]
    for j in range(n_blocks):
        rows = slice(j * ret_rows, (j + 1) * ret_rows)
        if has_state:
            s_cur = [s0_ref[j, hd] for hd in range(HEADS_R)]
        o, s_cur = _retention_block(qr[rows], kr[rows], vr[rows], gr[rows], s_cur, *tables)
        or_ref[rows, :] = o.astype(BF16)
        if has_state:
            for hd in range(HEADS_R):
                sout_ref[slot, j, hd] = _rebase_state(s_cur[hd], rc_ref, rs_ref)
        if pending:
            pending.pop(0)()
    for task in pending:
        task()
    if has_state:
        if slot > 0:
            sout_ref[0:slot] = sout_prev[...]
    else:
        for hd in range(HEADS_R):
            state[hd] = s_cur[hd]

        @pl.when(tile_in_seq == seq_tiles - 1)
        def _():
            if slot > 0:
                sout_ref[0:slot] = sout_prev[...]
            for hd in range(HEADS_R):
                sout_ref[slot, 0, hd] = _rebase_state(state[hd], rc_ref, rs_ref)

    ka, va = mixer_a["k"], mixer_a["v"]

    def write_tails():
        if slot > 0:
            kt_ref[0:slot] = kt_prev[...]
            vt_ref[0:slot] = vt_prev[...]
        if len(kt_ref.shape) == 4:
            kt_ref[slot, 0] = ka.T
            vt_ref[slot, 0] = va.T
            return
        for hd in range(HEADS_A):
            rows = pl.ds(hd, ka.shape[0], stride=HEADS_A)
            cols = slice(hd * HEAD_DIM_A, (hd + 1) * HEAD_DIM_A)
            kt_ref[slot, rows, :] = ka[:, cols]
            vt_ref[slot, rows, :] = va[:, cols]

    if tail_period == 1:
        write_tails()
    else:
        pl.when(pl.program_id(0) % tail_period >= tail_first)(write_tails)


RETENTION_BLOCK = 256


def _retention_tables(blk, seq):
    log_gamma = jnp.log(1.0 - 2.0 ** (-5.0 - jnp.arange(HEADS_R, dtype=F32)))
    idx = jnp.arange(blk, dtype=F32)
    diff = idx[:, None] - idx[None, :]
    decay = jnp.where(diff[None] >= 0,
                      jnp.exp(jnp.maximum(diff, 0.0)[None] * log_gamma[:, None, None]), 0.0)
    lanes = lambda t: jnp.repeat(t, HEAD_DIM_R, axis=-1)
    q_decay = lanes(jnp.exp((idx + 1.0)[:, None] * log_gamma[None, :]))
    k_decay = lanes(jnp.exp((blk - 1.0 - idx)[:, None] * log_gamma[None, :]))
    s_decay = jnp.broadcast_to(jnp.exp(blk * log_gamma)[:, None, None], (HEADS_R, 1, HEAD_DIM_R))
    half = HEAD_DIM_R // 2
    inv = ROPE_BASE ** (-jnp.arange(half, dtype=F32) / half)
    ang = jnp.asarray(-seq, dtype=F32) * inv
    reb_cos = jnp.broadcast_to(jnp.cos(ang)[:, None], (half, HEAD_DIM_R))
    reb_sin = jnp.broadcast_to(jnp.sin(ang)[:, None], (half, HEAD_DIM_R))
    return decay, q_decay, k_decay, s_decay, reb_cos, reb_sin


def _proj(x2d, prm, layer, seq, tail_rows, prev, state0=None, tm=512):
    t_rows = x2d.shape[0]
    assert t_rows % tm == 0
    n_tiles = t_rows // tm
    n_seq = t_rows // seq
    has_state = state0 is not None
    if seq >= tm:
        assert seq % tm == 0 and tail_rows % tm == 0 and not has_state
        tail_period = seq // tm
        tail_first = (seq - tail_rows) // tm
        table_rows = seq
        ret_rows = min(RETENTION_BLOCK, tm)
    else:
        assert tm % seq == 0 and tail_rows == seq and has_state
        tail_period, tail_first = 1, 0
        table_rows = tm
        ret_rows = seq
    assert tm % ret_rows == 0
    n_tab = table_rows // tm
    seqs_per_tile = max(1, tm // seq)
    decay, q_decay, k_decay, s_decay, reb_cos, reb_sin = _retention_tables(ret_rows, seq)

    half = HEAD_DIM_R // 2
    inv = ROPE_BASE ** (-jnp.arange(half, dtype=F32) / half)
    pos = (jnp.arange(table_rows) % seq).astype(F32)
    ang = pos[:, None] * inv
    cos, sin = jnp.cos(ang), jnp.sin(ang)
    cos_t = jnp.concatenate([cos, cos], axis=-1)
    sin_t = jnp.concatenate([-sin, sin], axis=-1)

    lane = np.arange(MXU_DIM)
    head_mean = jnp.asarray((lane[:, None] // HEAD_DIM_A == lane[None, :] // HEAD_DIM_A)
                            .astype(np.float32) / HEAD_DIM_A, dtype=BF16)

    row = lambda i: (i, 0)
    out_bf = jax.ShapeDtypeStruct((t_rows, SEG), BF16)
    slot = 0 if prev is None else prev[0].shape[0]
    state_dims = (seqs_per_tile, HEADS_R, HEAD_DIM_R, HEAD_DIM_R)
    state_full = (n_seq, HEADS_R, HEAD_DIM_R, HEAD_DIM_R)
    state_map = lambda i: (0, i // tail_period, 0, 0, 0)
    if seq >= tm:
        tail_full, tail_dims = (n_seq, SEG, tail_rows), (1, SEG, tm)
        tail_map = lambda i: (0, i // tail_period, 0, jnp.maximum(i % tail_period - tail_first, 0))
    else:
        tail_full, tail_dims = (t_rows * HEADS_A, HEAD_DIM_A), (tm * HEADS_A, HEAD_DIM_A)
        tail_map = lambda i: (0, i, 0)
    stacked = lambda n, dims, index_map: pl.BlockSpec((n,) + dims, index_map)
    tail_blk = stacked(slot + 1, tail_dims, tail_map)
    state_blk = stacked(slot + 1, state_dims, state_map)
    tail_shape = jax.ShapeDtypeStruct((slot + 1,) + tail_full, F32)
    state_shape = jax.ShapeDtypeStruct((slot + 1,) + state_full, F32)
    blk = pl.BlockSpec((tm, SEG), row)
    in_specs = [
        pl.BlockSpec((tm, D_MODEL), row),
        _layer_resident((1, D_MODEL), layer),
        _layer_resident((D_MODEL, N_SEG * SEG), layer),
        _layer_resident((1, SEG), layer),
        _layer_resident((1, SEG), layer),
        pl.BlockSpec((tm, HEAD_DIM_R), lambda i: (i % n_tab, 0)),
        pl.BlockSpec((tm, HEAD_DIM_R), lambda i: (i % n_tab, 0)),
        _resident((MXU_DIM, MXU_DIM)),
        _resident(decay.shape), _resident(q_decay.shape), _resident(k_decay.shape),
        _resident(s_decay.shape), _layer_resident((1, WIDTH_R), layer),
        _resident(reb_cos.shape), _resident(reb_sin.shape),
    ]
    args = [x2d, prm["norm_mix"], prm["w_in"], prm["q_norm"], prm["k_norm"], cos_t, sin_t, head_mean,
            decay, q_decay, k_decay, s_decay, prm["ret_norm"], reb_cos, reb_sin]
    scratch = []
    if has_state:
        in_specs.append(pl.BlockSpec((None,) + state_dims, lambda i: (layer, i, 0, 0, 0)))
        args.append(state0)
    else:
        scratch.append(pltpu.VMEM((HEADS_R, HEAD_DIM_R, HEAD_DIM_R), F32))
    if slot > 0:
        in_specs += [stacked(slot, tail_dims, tail_map), stacked(slot, tail_dims, tail_map),
                     stacked(slot, state_dims, state_map)]
        args += list(prev)
    return pl.pallas_call(
        functools.partial(_proj_kernel, slot=slot, tail_period=tail_period, tail_first=tail_first,
                          ret_rows=ret_rows, seq_tiles=tail_period, has_state=has_state),
        grid=(n_tiles,),
        in_specs=in_specs,
        out_specs=[blk, blk, blk, blk, tail_blk, tail_blk, state_blk],
        out_shape=[out_bf, out_bf, out_bf, out_bf, tail_shape, tail_shape, state_shape],
        scratch_shapes=scratch,
        compiler_params=_params(1),
        name="proj",
    )(*args)


NT_DIMS = (((1,), (1,)), ((), ()))
ATTN_SCORES_AHEAD = 2


def _pair_scores(tile, pair, bias_ref):
    q_pair = tile["q"](pair)
    tq = q_pair.shape[0]
    lane = lax.broadcasted_iota(jnp.int32, (tq, LANES), 1)
    zero = jnp.zeros_like(q_pair)
    qz = jnp.concatenate([jnp.where(lane < HEAD_DIM_A, q_pair, zero),
                          jnp.where(lane >= HEAD_DIM_A, q_pair, zero)], axis=0)
    return tile["qk"](pair, qz) + bias_ref[pair, :, tile["lo"]:]


def _pair_output(tile, pair, s):
    tq = s.shape[0] // 2
    m = jnp.max(s, axis=-1, keepdims=True)
    p = jnp.exp2(s - m)
    denom = jnp.sum(p, axis=-1, keepdims=True)
    r = tile["pv"](pair, p.astype(BF16)) / denom
    lane = lax.broadcasted_iota(jnp.int32, (tq, LANES), 1)
    return jnp.where(lane < HEAD_DIM_A, r[:tq], r[tq:])


def _attention_units(tiles, bias_ref):
    n_pairs = HEADS_A // 2
    units = [(tile, pair) for tile in tiles for pair in range(n_pairs)]
    ahead = [_pair_scores(*unit, bias_ref) for unit in units[:ATTN_SCORES_AHEAD]]
    yield
    outs = []
    for idx, (tile, pair) in enumerate(units):
        s_cur = ahead.pop(0)
        if idx + ATTN_SCORES_AHEAD < len(units):
            ahead.append(_pair_scores(*units[idx + ATTN_SCORES_AHEAD], bias_ref))
        outs.append(_pair_output(tile, pair, s_cur))
        if pair == n_pairs - 1:
            tile["store"](jnp.concatenate(outs, axis=-1).astype(BF16))
            outs = []
        yield


def _run_attention_tiles(tiles, bias_ref):
    for _ in _attention_units(tiles, bias_ref):
        pass


def _attn_prompt_kernel(q_ref, k_ref, v_ref, bias_ref, o_ref, *, tq, tps, n_steps):
    nk = bias_ref.shape[-1]
    n_special = BAND_ROWS // tq
    step = pl.program_id(1)
    lanes = lambda pair: pl.ds(pair * LANES, LANES)

    def make_tile(i, row0, nkv):
        rows = pl.ds(row0, nkv)
        qrows = pl.ds(i * tq, tq)

        def store(o):
            o_ref[0, qrows, :] = o

        def qk(pair, qz):
            return lax.dot_general(qz, k_ref[0, rows, lanes(pair)], NT_DIMS, preferred_element_type=F32)

        def pv(pair, p):
            return jnp.dot(p, v_ref[0, rows, lanes(pair)], preferred_element_type=F32)
        return dict(q=lambda pair: q_ref[0, qrows, lanes(pair)], qk=qk, pv=pv, lo=nk - nkv, store=store)

    for j in range(min(n_special // tps, n_steps)):
        @pl.when(step == j)
        def _(j=j):
            _run_attention_tiles(
                [make_tile(i, 0, (j * tps + i + 1) * tq) for i in range(tps)], bias_ref)

    if n_steps > n_special // tps:
        @pl.when(step >= n_special // tps)
        def _():
            tiles = []
            for i in range(tps):
                row0 = pl.multiple_of((step * tps + i) * tq - BAND_ROWS, tq)
                tiles.append(make_tile(i, row0, nk))
            _run_attention_tiles(tiles, bias_ref)


def _attn_cache_kernel(q_ref, k_ref, v_ref, bias_ref, ck_ref, cv_ref, o_ref, *, bb):
    lanes = lambda pair: pl.ds(pair * LANES, LANES)

    def make_tile(bi):
        past = lambda ref, pair: ref[bi, lanes(pair), :].astype(BF16)

        def store(o):
            o_ref[bi] = o

        def qk(pair, qz):
            s_past = jnp.dot(qz, past(ck_ref, pair), preferred_element_type=F32)
            s_new = lax.dot_general(qz, k_ref[bi, :, lanes(pair)], NT_DIMS, preferred_element_type=F32)
            return jnp.concatenate([s_past, s_new], axis=-1)

        def pv(pair, p):
            return (lax.dot_general(p[:, :BAND_ROWS], past(cv_ref, pair), NT_DIMS, preferred_element_type=F32)
                    + jnp.dot(p[:, BAND_ROWS:], v_ref[bi, :, lanes(pair)], preferred_element_type=F32))
        return dict(q=lambda pair: q_ref[bi, :, lanes(pair)], qk=qk, pv=pv, lo=0, store=store)

    _run_attention_tiles([make_tile(bi) for bi in range(bb)], bias_ref)


def _band_bias(rel_table, chunks_per_tile):
    n_band = (N_PREV_CHUNKS + 1) * CHUNK
    table = rel_table.astype(F32)
    lead = table.shape[:-1]
    u_min = BAND_ROWS + REL_CLIP - (n_band - 1)
    n_far = CHUNK - 1 + BAND_ROWS - REL_CLIP
    g = jnp.concatenate([table[..., u_min:], jnp.broadcast_to(table[..., -1:], lead + (n_far,))], axis=-1)
    r = jnp.concatenate([g[..., ::-1], jnp.zeros(lead + (1,), F32)], axis=-1)
    period = r.shape[-1]
    tiled = jnp.tile(r, (1,) * len(lead) + (CHUNK + 1,))[..., :CHUNK * (period + 1)]
    hankel = tiled.reshape(lead + (CHUNK, period + 1))[..., :n_band]
    base = hankel[..., ::-1, :]
    pad_lead = ((0, 0),) * (len(lead) + 1)
    rows = [jnp.pad(base, pad_lead + ((i * CHUNK, (chunks_per_tile - 1 - i) * CHUNK),),
                    constant_values=NEG_INF) for i in range(chunks_per_tile)]
    bias = jnp.concatenate(rows, axis=-2) * LOG2_E
    return bias.reshape(lead[:-1] + (HEADS_A // 2, 2 * chunks_per_tile * CHUNK, bias.shape[-1]))


ATTN_CHUNKS_PER_TILE = 2
ATTN_TILES_PER_STEP = 4
ATTN_CACHE_BATCH = 4


def _attention(qa, ka, va, bias, layer, cache_k=None, cache_v=None):
    b, seq, _ = qa.shape
    out_shape = jax.ShapeDtypeStruct((b, seq, WIDTH_A), BF16)
    bias_spec = _layer_resident(bias.shape[1:], layer)
    if cache_k is None:
        tps = ATTN_TILES_PER_STEP
        tq = ATTN_CHUNKS_PER_TILE * CHUNK
        assert seq % (tq * tps) == 0 and (BAND_ROWS // tq) % tps == 0
        n_steps = seq // (tq * tps)
        tile = pl.BlockSpec((1, tq * tps, WIDTH_A), lambda i, t: (i, t, 0))
        whole = pl.BlockSpec((1, seq, WIDTH_A), lambda i, t: (i, 0, 0))
        return pl.pallas_call(
            functools.partial(_attn_prompt_kernel, tq=tq, tps=tps, n_steps=n_steps),
            grid=(b, n_steps),
            in_specs=[tile, whole, whole, bias_spec],
            out_specs=tile,
            out_shape=out_shape,
            compiler_params=_params(2),
            name="band_attention",
        )(qa, ka, va, bias)
    assert seq == CHUNK
    bb = ATTN_CACHE_BATCH
    assert b % bb == 0
    new = pl.BlockSpec((bb, seq, WIDTH_A), lambda i: (i, 0, 0))
    cache = pl.BlockSpec((None, bb, WIDTH_A, BAND_ROWS), lambda i: (layer, i, 0, 0))
    return pl.pallas_call(
        functools.partial(_attn_cache_kernel, bb=bb),
        grid=(b // bb,),
        in_specs=[new, new, new, bias_spec, cache, cache],
        out_specs=new,
        out_shape=out_shape,
        compiler_params=_params(1),
        name="band_attention_cache",
    )(qa, ka, va, bias, cache_k, cache_v)


def _ff_chunks(d_ff):
    n_tiles = d_ff // MXU_DIM
    first = (n_tiles + 1) // 2 * MXU_DIM
    return [c for c in (first, d_ff - first) if c > 0] if n_tiles > 1 else [d_ff]


def _merge_rows(x, o_a, o_r, wo_ref, g_ref):
    mix = jnp.concatenate([o_a, o_r], axis=-1)
    x = x + jnp.dot(mix, wo_ref[...], preferred_element_type=F32)
    ms = jnp.mean(x * x, axis=-1, keepdims=True)
    return x, ((x * lax.rsqrt(ms + RMS_EPS)) * g_ref[...]).astype(BF16)


def _ffn_piece(h, y, wg_ref, wu_ref, wd_ref, lo, width):
    sl = slice(lo, lo + width)
    gate = jnp.dot(h, wg_ref[:, sl], preferred_element_type=F32)
    up = jnp.dot(h, wu_ref[:, sl], preferred_element_type=F32)
    act = ((gate * jax.nn.sigmoid(gate)) * up).astype(BF16)
    return y + jnp.dot(act, wd_ref[sl, :], preferred_element_type=F32)


def _merge_ffn_kernel(x_ref, oa_ref, or_ref, wo_ref, g_ref, wg_ref, wu_ref, wd_ref, y_ref):
    y, h = _merge_rows(x_ref[...], oa_ref[...], or_ref[...], wo_ref, g_ref)
    lo = 0
    for width in _ff_chunks(wg_ref.shape[1]):
        y = _ffn_piece(h, y, wg_ref, wu_ref, wd_ref, lo, width)
        lo += width
    y_ref[...] = y


def _merge_ffn(x2d, o_a, o_r, prm, layer, tm=512):
    t_rows = x2d.shape[0]
    d_ff = prm["w_gate"].shape[-1]
    assert t_rows % tm == 0
    row = lambda i: (i, 0)
    return pl.pallas_call(
        _merge_ffn_kernel,
        grid=(t_rows // tm,),
        in_specs=[
            pl.BlockSpec((tm, D_MODEL), row),
            pl.BlockSpec((tm, WIDTH_A), row),
            pl.BlockSpec((tm, WIDTH_R), row),
            _layer_resident((D_MODEL, D_MODEL), layer),
            _layer_resident((1, D_MODEL), layer),
            _layer_resident((D_MODEL, d_ff), layer),
            _layer_resident((D_MODEL, d_ff), layer),
            _layer_resident((d_ff, D_MODEL), layer),
        ],
        out_specs=pl.BlockSpec((tm, D_MODEL), row),
        out_shape=jax.ShapeDtypeStruct((t_rows, D_MODEL), F32),
        compiler_params=_params(1),
        name="merge_ffn",
    )(x2d, o_a, o_r, prm["w_out"], prm["norm_ffn"], prm["w_gate"], prm["w_up"], prm["w_down"])


def _layer(x, prm, layer, bias, tail_rows, carried, cache_k=None, cache_v=None, state0=None):
    b, seq, _ = x.shape
    x2d = x.reshape(b * seq, D_MODEL)
    qa, ka, va, o_r, *carried = _proj(x2d, prm, layer, seq, tail_rows, carried, state0)
    r3 = lambda t: t.reshape(b, seq, SEG)
    o_a = _attention(r3(qa), r3(ka), r3(va), bias, layer, cache_k, cache_v)
    y = _merge_ffn(x2d, o_a.reshape(b * seq, WIDTH_A), o_r, prm, layer)
    return y.reshape(b, seq, D_MODEL), carried


def _tail_heads(t, b, tail_rows):
    depth = t.shape[0]
    if t.ndim == 4:
        return t.reshape(depth, b, HEADS_A, HEAD_DIM_A, tail_rows).transpose(0, 1, 4, 2, 3)
    return t.reshape(depth, b, tail_rows, HEADS_A, HEAD_DIM_A)


def kernel(x_prompt, x_sample, cache_a_k, cache_a_v, state_ret, norm_mix, w_in, q_norm, k_norm,
           rel_table, ret_norm, w_out, norm_ffn, w_gate, w_up, w_down):
    depth = w_in.shape[0]
    bp, sq_p, _ = x_prompt.shape
    bs, sq_s, _ = x_sample.shape
    assert cache_a_k.shape[2] == BAND_ROWS
    rows_p = min(BAND_ROWS, sq_p)

    vec = lambda t: t.reshape(depth, 1, -1)
    prm = dict(
        norm_mix=vec(norm_mix), norm_ffn=vec(norm_ffn), ret_norm=vec(ret_norm),
        q_norm=vec(jnp.tile(q_norm, (1, HEADS_A))), k_norm=vec(jnp.tile(k_norm, (1, HEADS_A))),
        w_in=w_in.astype(BF16), w_out=w_out.astype(BF16),
        w_gate=w_gate.astype(BF16), w_up=w_up.astype(BF16), w_down=w_down.astype(BF16))
    bias_p = _band_bias(rel_table, ATTN_CHUNKS_PER_TILE)
    bias_s = _band_bias(rel_table, 1)
    feature_major = lambda c: c.transpose(0, 1, 3, 4, 2).reshape(depth, bs, WIDTH_A, BAND_ROWS)
    cache_k, cache_v = feature_major(cache_a_k), feature_major(cache_a_v)

    xp, xs = x_prompt, x_sample
    out_p, outs_s = None, []
    for l in range(depth):
        xp, out_p = _layer(xp, prm, l, bias_p, rows_p, out_p)
        xs, out_s = _layer(xs, prm, l, bias_s, sq_s, None, cache_k, cache_v, state_ret)
        outs_s.append(out_s)
    out_s = [jnp.concatenate(parts, axis=0) for parts in zip(*outs_s)]
    return (xp, xs, _tail_heads(out_p[0], bp, rows_p), _tail_heads(out_p[1], bp, rows_p), out_p[2],
            _tail_heads(out_s[0], bs, sq_s), _tail_heads(out_s[1], bs, sq_s), out_s[2])
```

```python
import functools

import jax
import jax.numpy as jnp
import numpy as np
from jax import lax
from jax.experimental import pallas as pl
from jax.experimental.pallas import tpu as pltpu

F32 = jnp.float32
BF16 = jnp.bfloat16

D_MODEL = 1024
CHUNK = 64
N_PREV_CHUNKS = 8
BAND_ROWS = N_PREV_CHUNKS * CHUNK
HEADS_A = 8
HEAD_DIM_A = 64
WIDTH_A = HEADS_A * HEAD_DIM_A
HEADS_R = 4
HEAD_DIM_R = 128
WIDTH_R = HEADS_R * HEAD_DIM_R
N_SEG = 7
SEG = 512
REL_CLIP = 2 * CHUNK
ROPE_BASE = 10000.0
RMS_EPS = 1e-6
GN_EPS = 1e-5
NEG_INF = -1e30
LOG2_E = 1.4426950408889634

VMEM_LIMIT_BYTES = 56 * 1024 * 1024
LANES = 128
MXU_DIM = 256


def _resident(shape):
    nd = len(shape)
    return pl.BlockSpec(shape, lambda *_: (0,) * nd, pipeline_mode=pl.Buffered(1))


def _layer_resident(shape, layer):
    nd = len(shape)
    return pl.BlockSpec((None,) + tuple(shape), lambda *_: (layer,) + (0,) * nd,
                        pipeline_mode=pl.Buffered(1))


def _params(n_axes):
    return pltpu.CompilerParams(dimension_semantics=("arbitrary",) * n_axes,
                                vmem_limit_bytes=VMEM_LIMIT_BYTES)


def _retention_block(q, k, v, g, s_prev, dec_ref, qd_ref, kd_ref, sd_ref, rn_ref):
    outs, s_new = [], []
    for hd in range(HEADS_R):
        sl = slice(hd * HEAD_DIM_R, (hd + 1) * HEAD_DIM_R)
        qh, kh, vh = q[:, sl], k[:, sl], v[:, sl]
        scores = lax.dot_general(qh, kh.astype(BF16), NT_DIMS, preferred_element_type=F32) * dec_ref[hd]
        intra = jnp.dot(scores.astype(BF16), vh, preferred_element_type=F32)
        cross = jnp.dot(qh, s_prev[hd].astype(BF16), preferred_element_type=F32) * qd_ref[:, sl]
        k_dec = (kh * kd_ref[:, sl]).astype(BF16)
        s_new.append(sd_ref[hd] * s_prev[hd] + lax.dot_general(
            k_dec, vh, (((0,), (0,)), ((), ())), preferred_element_type=F32))
        o = intra + cross
        mu = jnp.mean(o, axis=-1, keepdims=True)
        d = o - mu
        var = jnp.mean(d * d, axis=-1, keepdims=True)
        outs.append(d * lax.rsqrt(var + GN_EPS))
    o_norm = jnp.concatenate(outs, axis=-1) * rn_ref[...]
    return (g * jax.nn.sigmoid(g)) * o_norm, s_new


def _rebase_state(s, rc_ref, rs_ref):
    half = HEAD_DIM_R // 2
    s1, s2 = s[:half], s[half:]
    return jnp.concatenate([s1 * rc_ref[...] - s2 * rs_ref[...],
                            s1 * rs_ref[...] + s2 * rc_ref[...]], axis=0)


def _proj_kernel(*refs, slot, tail_period, tail_first, ret_rows, seq_tiles, has_state):
    (x_ref, g_ref, w_ref, qn_ref, kn_ref, cos_ref, sin_ref, hm_ref,
     dec_ref, qd_ref, kd_ref, sd_ref, rn_ref, rc_ref, rs_ref) = refs[:15]
    n_in = 15
    if has_state:
        s0_ref = refs[n_in]
        n_in += 1
    if slot > 0:
        kt_prev, vt_prev, sout_prev = refs[n_in:n_in + 3]
        n_in += 3
    qa_ref, ka_ref, va_ref, or_ref, kt_ref, vt_ref, sout_ref = refs[n_in:n_in + 7]
    if not has_state:
        tile_in_seq = pl.program_id(0) % seq_tiles

        @pl.when(tile_in_seq == 0)
        def _():
            refs[-1][...] = jnp.zeros(refs[-1].shape, F32)
    x = x_ref[...]
    ms = jnp.mean(x * x, axis=-1, keepdims=True)
    h = ((x * lax.rsqrt(ms + RMS_EPS)) * g_ref[...]).astype(BF16)

    def seg(j):
        return jnp.dot(h, w_ref[:, j * SEG:(j + 1) * SEG], preferred_element_type=F32)

    def head_rms(z, gain):
        z2 = (z * z).astype(BF16)
        parts = [jnp.dot(z2[:, c * MXU_DIM:(c + 1) * MXU_DIM], hm_ref[...], preferred_element_type=F32)
                 for c in range(SEG // MXU_DIM)]
        msq = jnp.concatenate(parts, axis=-1)
        return (z * lax.rsqrt(msq + RMS_EPS)) * gain

    def rotary(z):
        outs = []
        for hd in range(HEADS_R):
            zh = z[:, hd * HEAD_DIM_R:(hd + 1) * HEAD_DIM_R]
            outs.append(zh * cos_ref[...] + pltpu.roll(zh, HEAD_DIM_R // 2, axis=1) * sin_ref[...])
        return jnp.concatenate(outs, axis=-1)

    mixer_a = {}

    def project_qa():
        qa_ref[...] = (head_rms(seg(0), qn_ref[...]) * (HEAD_DIM_A ** -0.5 * LOG2_E)).astype(BF16)

    def project_ka():
        mixer_a["k"] = head_rms(seg(1), kn_ref[...])
        ka_ref[...] = mixer_a["k"].astype(BF16)

    def project_va():
        mixer_a["v"] = seg(2)
        va_ref[...] = mixer_a["v"].astype(BF16)

    pending = [project_qa, project_ka, project_va]

    qr = rotary(seg(3)).astype(BF16)
    kr = rotary(seg(4)) * (HEAD_DIM_R ** -0.5)
    vr = seg(5).astype(BF16)
    gr = seg(6)
    tables = (dec_ref, qd_ref, kd_ref, sd_ref, rn_ref)
    n_blocks = x.shape[0] // ret_rows
    if not has_state:
        state = refs[-1]
        s_cur = [state[hd] for hd in range(HEADS_R)]
    for j in range(n_blocks):
        rows = slice(j * ret_rows, (j + 1) * ret_rows)
        if has_state:
            s_cur = [s0_ref[j, hd] for hd in range(HEADS_R)]
        o, s_cur = _retention_block(qr[rows], kr[rows], vr[rows], gr[rows], s_cur, *tables)
        or_ref[rows, :] = o.astype(BF16)
        if has_state:
            for hd in range(HEADS_R):
                sout_ref[slot, j, hd] = _rebase_state(s_cur[hd], rc_ref, rs_ref)
        if pending:
            pending.pop(0)()
    for task in pending:
        task()
    if has_state:
        if slot > 0:
            sout_ref[0:slot] = sout_prev[...]
    else:
        for hd in range(HEADS_R):
            state[hd] = s_cur[hd]

        @pl.when(tile_in_seq == seq_tiles - 1)
        def _():
            if slot > 0:
                sout_ref[0:slot] = sout_prev[...]
            for hd in range(HEADS_R):
                sout_ref[slot, 0, hd] = _rebase_state(state[hd], rc_ref, rs_ref)

    ka, va = mixer_a["k"], mixer_a["v"]

    def write_tails():
        if slot > 0:
            kt_ref[0:slot] = kt_prev[...]
            vt_ref[0:slot] = vt_prev[...]
        if len(kt_ref.shape) == 4:
            kt_ref[slot, 0] = ka.T
            vt_ref[slot, 0] = va.T
            return
        for hd in range(HEADS_A):
            rows = pl.ds(hd, ka.shape[0], stride=HEADS_A)
            cols = slice(hd * HEAD_DIM_A, (hd + 1) * HEAD_DIM_A)
            kt_ref[slot, rows, :] = ka[:, cols]
            vt_ref[slot, rows, :] = va[:, cols]

    if tail_period == 1:
        write_tails()
    else:
        pl.when(pl.program_id(0) % tail_period >= tail_first)(write_tails)


RETENTION_BLOCK = 256


def _retention_tables(blk, seq):
    log_gamma = jnp.log(1.0 - 2.0 ** (-5.0 - jnp.arange(HEADS_R, dtype=F32)))
    idx = jnp.arange(blk, dtype=F32)
    diff = idx[:, None] - idx[None, :]
    decay = jnp.where(diff[None] >= 0,
                      jnp.exp(jnp.maximum(diff, 0.0)[None] * log_gamma[:, None, None]), 0.0)
    lanes = lambda t: jnp.repeat(t, HEAD_DIM_R, axis=-1)
    q_decay = lanes(jnp.exp((idx + 1.0)[:, None] * log_gamma[None, :]))
    k_decay = lanes(jnp.exp((blk - 1.0 - idx)[:, None] * log_gamma[None, :]))
    s_decay = jnp.broadcast_to(jnp.exp(blk * log_gamma)[:, None, None], (HEADS_R, 1, HEAD_DIM_R))
    half = HEAD_DIM_R // 2
    inv = ROPE_BASE ** (-jnp.arange(half, dtype=F32) / half)
    ang = jnp.asarray(-seq, dtype=F32) * inv
    reb_cos = jnp.broadcast_to(jnp.cos(ang)[:, None], (half, HEAD_DIM_R))
    reb_sin = jnp.broadcast_to(jnp.sin(ang)[:, None], (half, HEAD_DIM_R))
    return decay, q_decay, k_decay, s_decay, reb_cos, reb_sin


def _proj(x2d, prm, layer, seq, tail_rows, prev, state0=None, tm=512):
    t_rows = x2d.shape[0]
    assert t_rows % tm == 0
    n_tiles = t_rows // tm
    n_seq = t_rows // seq
    has_state = state0 is not None
    if seq >= tm:
        assert seq % tm == 0 and tail_rows % tm == 0 and not has_state
        tail_period = seq // tm
        tail_first = (seq - tail_rows) // tm
        table_rows = seq
        ret_rows = min(RETENTION_BLOCK, tm)
    else:
        assert tm % seq == 0 and tail_rows == seq and has_state
        tail_period, tail_first = 1, 0
        table_rows = tm
        ret_rows = seq
    assert tm % ret_rows == 0
    n_tab = table_rows // tm
    seqs_per_tile = max(1, tm // seq)
    decay, q_decay, k_decay, s_decay, reb_cos, reb_sin = _retention_tables(ret_rows, seq)

    half = HEAD_DIM_R // 2
    inv = ROPE_BASE ** (-jnp.arange(half, dtype=F32) / half)
    pos = (jnp.arange(table_rows) % seq).astype(F32)
    ang = pos[:, None] * inv
    cos, sin = jnp.cos(ang), jnp.sin(ang)
    cos_t = jnp.concatenate([cos, cos], axis=-1)
    sin_t = jnp.concatenate([-sin, sin], axis=-1)

    lane = np.arange(MXU_DIM)
    head_mean = jnp.asarray((lane[:, None] // HEAD_DIM_A == lane[None, :] // HEAD_DIM_A)
                            .astype(np.float32) / HEAD_DIM_A, dtype=BF16)

    row = lambda i: (i, 0)
    out_bf = jax.ShapeDtypeStruct((t_rows, SEG), BF16)
    slot = 0 if prev is None else prev[0].shape[0]
    state_dims = (seqs_per_tile, HEADS_R, HEAD_DIM_R, HEAD_DIM_R)
    state_full = (n_seq, HEADS_R, HEAD_DIM_R, HEAD_DIM_R)
    state_map = lambda i: (0, i // tail_period, 0, 0, 0)
    if seq >= tm:
        tail_full, tail_dims = (n_seq, SEG, tail_rows), (1, SEG, tm)
        tail_map = lambda i: (0, i // tail_period, 0, jnp.maximum(i % tail_period - tail_first, 0))
    else:
        tail_full, tail_dims = (t_rows * HEADS_A, HEAD_DIM_A), (tm * HEADS_A, HEAD_DIM_A)
        tail_map = lambda i: (0, i, 0)
    stacked = lambda n, dims, index_map: pl.BlockSpec((n,) + dims, index_map)
    tail_blk = stacked(slot + 1, tail_dims, tail_map)
    state_blk = stacked(slot + 1, state_dims, state_map)
    tail_shape = jax.ShapeDtypeStruct((slot + 1,) + tail_full, F32)
    state_shape = jax.ShapeDtypeStruct((slot + 1,) + state_full, F32)
    blk = pl.BlockSpec((tm, SEG), row)
    in_specs = [
        pl.BlockSpec((tm, D_MODEL), row),
        _layer_resident((1, D_MODEL), layer),
        _layer_resident((D_MODEL, N_SEG * SEG), layer),
        _layer_resident((1, SEG), layer),
        _layer_resident((1, SEG), layer),
        pl.BlockSpec((tm, HEAD_DIM_R), lambda i: (i % n_tab, 0)),
        pl.BlockSpec((tm, HEAD_DIM_R), lambda i: (i % n_tab, 0)),
        _resident((MXU_DIM, MXU_DIM)),
        _resident(decay.shape), _resident(q_decay.shape), _resident(k_decay.shape),
        _resident(s_decay.shape), _layer_resident((1, WIDTH_R), layer),
        _resident(reb_cos.shape), _resident(reb_sin.shape),
    ]
    args = [x2d, prm["norm_mix"], prm["w_in"], prm["q_norm"], prm["k_norm"], cos_t, sin_t, head_mean,
            decay, q_decay, k_decay, s_decay, prm["ret_norm"], reb_cos, reb_sin]
    scratch = []
    if has_state:
        in_specs.append(pl.BlockSpec((None,) + state_dims, lambda i: (layer, i, 0, 0, 0)))
        args.append(state0)
    else:
        scratch.append(pltpu.VMEM((HEADS_R, HEAD_DIM_R, HEAD_DIM_R), F32))
    if slot > 0:
        in_specs += [stacked(slot, tail_dims, tail_map), stacked(slot, tail_dims, tail_map),
                     stacked(slot, state_dims, state_map)]
        args += list(prev)
    return pl.pallas_call(
        functools.partial(_proj_kernel, slot=slot, tail_period=tail_period, tail_first=tail_first,
                          ret_rows=ret_rows, seq_tiles=tail_period, has_state=has_state),
        grid=(n_tiles,),
        in_specs=in_specs,
        out_specs=[blk, blk, blk, blk, tail_blk, tail_blk, state_blk],
        out_shape=[out_bf, out_bf, out_bf, out_bf, tail_shape, tail_shape, state_shape],
        scratch_shapes=scratch,
        compiler_params=_params(1),
        name="proj",
    )(*args)


NT_DIMS = (((1,), (1,)), ((), ()))
ATTN_SCORES_AHEAD = 2


def _pair_scores(tile, pair, bias_ref):
    q_pair = tile["q"](pair)
    tq = q_pair.shape[0]
    lane = lax.broadcasted_iota(jnp.int32, (tq, LANES), 1)
    zero = jnp.zeros_like(q_pair)
    qz = jnp.concatenate([jnp.where(lane < HEAD_DIM_A, q_pair, zero),
                          jnp.where(lane >= HEAD_DIM_A, q_pair, zero)], axis=0)
    return tile["qk"](pair, qz) + bias_ref[pair, :, tile["lo"]:]


def _pair_output(tile, pair, s):
    tq = s.shape[0] // 2
    m = jnp.max(s, axis=-1, keepdims=True)
    p = jnp.exp2(s - m)
    denom = jnp.sum(p, axis=-1, keepdims=True)
    r = tile["pv"](pair, p.astype(BF16)) / denom
    lane = lax.broadcasted_iota(jnp.int32, (tq, LANES), 1)
    return jnp.where(lane < HEAD_DIM_A, r[:tq], r[tq:])


def _attention_units(tiles, bias_ref):
    n_pairs = HEADS_A // 2
    units = [(tile, pair) for tile in tiles for pair in range(n_pairs)]
    ahead = [_pair_scores(*unit, bias_ref) for unit in units[:ATTN_SCORES_AHEAD]]
    yield
    outs = []
    for idx, (tile, pair) in enumerate(units):
        s_cur = ahead.pop(0)
        if idx + ATTN_SCORES_AHEAD < len(units):
            ahead.append(_pair_scores(*units[idx + ATTN_SCORES_AHEAD], bias_ref))
        outs.append(_pair_output(tile, pair, s_cur))
        if pair == n_pairs - 1:
            tile["store"](jnp.concatenate(outs, axis=-1).astype(BF16))
            outs = []
        yield


def _run_attention_tiles(tiles, bias_ref):
    for _ in _attention_units(tiles, bias_ref):
        pass


def _attn_prompt_kernel(q_ref, k_ref, v_ref, bias_ref, o_ref, *, tq, tps, n_steps):
    nk = bias_ref.shape[-1]
    n_special = BAND_ROWS // tq
    step = pl.program_id(1)
    lanes = lambda pair: pl.ds(pair * LANES, LANES)

    def make_tile(i, row0, nkv):
        rows = pl.ds(row0, nkv)
        qrows = pl.ds(i * tq, tq)

        def store(o):
            o_ref[0, qrows, :] = o

        def qk(pair, qz):
            return lax.dot_general(qz, k_ref[0, rows, lanes(pair)], NT_DIMS, preferred_element_type=F32)

        def pv(pair, p):
            return jnp.dot(p, v_ref[0, rows, lanes(pair)], preferred_element_type=F32)
        return dict(q=lambda pair: q_ref[0, qrows, lanes(pair)], qk=qk, pv=pv, lo=nk - nkv, store=store)

    for j in range(min(n_special // tps, n_steps)):
        @pl.when(step == j)
        def _(j=j):
            _run_attention_tiles(
                [make_tile(i, 0, (j * tps + i + 1) * tq) for i in range(tps)], bias_ref)

    if n_steps > n_special // tps:
        @pl.when(step >= n_special // tps)
        def _():
            tiles = []
            for i in range(tps):
                row0 = pl.multiple_of((step * tps + i) * tq - BAND_ROWS, tq)
                tiles.append(make_tile(i, row0, nk))
            _run_attention_tiles(tiles, bias_ref)


def _attn_cache_kernel(q_ref, k_ref, v_ref, bias_ref, ck_ref, cv_ref, o_ref, *, bb):
    lanes = lambda pair: pl.ds(pair * LANES, LANES)

    def make_tile(bi):
        past = lambda ref, pair: ref[bi, lanes(pair), :].astype(BF16)

        def store(o):
            o_ref[bi] = o

        def qk(pair, qz):
            s_past = jnp.dot(qz, past(ck_ref, pair), preferred_element_type=F32)
            s_new = lax.dot_general(qz, k_ref[bi, :, lanes(pair)], NT_DIMS, preferred_element_type=F32)
            return jnp.concatenate([s_past, s_new], axis=-1)

        def pv(pair, p):
            return (lax.dot_general(p[:, :BAND_ROWS], past(cv_ref, pair), NT_DIMS, preferred_element_type=F32)
                    + jnp.dot(p[:, BAND_ROWS:], v_ref[bi, :, lanes(pair)], preferred_element_type=F32))
        return dict(q=lambda pair: q_ref[bi, :, lanes(pair)], qk=qk, pv=pv, lo=0, store=store)

    _run_attention_tiles([make_tile(bi) for bi in range(bb)], bias_ref)


def _band_bias(rel_table, chunks_per_tile):
    n_band = (N_PREV_CHUNKS + 1) * CHUNK
    table = rel_table.astype(F32)
    lead = table.shape[:-1]
    u_min = BAND_ROWS + REL_CLIP - (n_band - 1)
    n_far = CHUNK - 1 + BAND_ROWS - REL_CLIP
    g = jnp.concatenate([table[..., u_min:], jnp.broadcast_to(table[..., -1:], lead + (n_far,))], axis=-1)
    r = jnp.concatenate([g[..., ::-1], jnp.zeros(lead + (1,), F32)], axis=-1)
    period = r.shape[-1]
    tiled = jnp.tile(r, (1,) * len(lead) + (CHUNK + 1,))[..., :CHUNK * (period + 1)]
    hankel = tiled.reshape(lead + (CHUNK, period + 1))[..., :n_band]
    base = hankel[..., ::-1, :]
    pad_lead = ((0, 0),) * (len(lead) + 1)
    rows = [jnp.pad(base, pad_lead + ((i * CHUNK, (chunks_per_tile - 1 - i) * CHUNK),),
                    constant_values=NEG_INF) for i in range(chunks_per_tile)]
    bias = jnp.concatenate(rows, axis=-2) * LOG2_E
    return bias.reshape(lead[:-1] + (HEADS_A // 2, 2 * chunks_per_tile * CHUNK, bias.shape[-1]))


ATTN_CHUNKS_PER_TILE = 2
ATTN_TILES_PER_STEP = 4
ATTN_CACHE_BATCH = 4


def _attention(qa, ka, va, bias, layer, cache_k=None, cache_v=None):
    b, seq, _ = qa.shape
    out_shape = jax.ShapeDtypeStruct((b, seq, WIDTH_A), BF16)
    bias_spec = _layer_resident(bias.shape[1:], layer)
    if cache_k is None:
        tps = ATTN_TILES_PER_STEP
        tq = ATTN_CHUNKS_PER_TILE * CHUNK
        assert seq % (tq * tps) == 0 and (BAND_ROWS // tq) % tps == 0
        n_steps = seq // (tq * tps)
        tile = pl.BlockSpec((1, tq * tps, WIDTH_A), lambda i, t: (i, t, 0))
        whole = pl.BlockSpec((1, seq, WIDTH_A), lambda i, t: (i, 0, 0))
        return pl.pallas_call(
            functools.partial(_attn_prompt_kernel, tq=tq, tps=tps, n_steps=n_steps),
            grid=(b, n_steps),
            in_specs=[tile, whole, whole, bias_spec],
            out_specs=tile,
            out_shape=out_shape,
            compiler_params=_params(2),
            name="band_attention",
        )(qa, ka, va, bias)
    assert seq == CHUNK
    bb = ATTN_CACHE_BATCH
    assert b % bb == 0
    new = pl.BlockSpec((bb, seq, WIDTH_A), lambda i: (i, 0, 0))
    cache = pl.BlockSpec((None, bb, WIDTH_A, BAND_ROWS), lambda i: (layer, i, 0, 0))
    return pl.pallas_call(
        functools.partial(_attn_cache_kernel, bb=bb),
        grid=(b // bb,),
        in_specs=[new, new, new, bias_spec, cache, cache],
        out_specs=new,
        out_shape=out_shape,
        compiler_params=_params(1),
        name="band_attention_cache",
    )(qa, ka, va, bias, cache_k, cache_v)


def _ff_chunks(d_ff):
    n_tiles = d_ff // MXU_DIM
    first = (n_tiles + 1) // 2 * MXU_DIM
    return [c for c in (first, d_ff - first) if c > 0] if n_tiles > 1 else [d_ff]


def _merge_rows(x, o_a, o_r, wo_ref, g_ref):
    mix = jnp.concatenate([o_a, o_r], axis=-1)
    x = x + jnp.dot(mix, wo_ref[...], preferred_element_type=F32)
    ms = jnp.mean(x * x, axis=-1, keepdims=True)
    return x, ((x * lax.rsqrt(ms + RMS_EPS)) * g_ref[...]).astype(BF16)


def _ffn_piece(h, y, wg_ref, wu_ref, wd_ref, lo, width):
    sl = slice(lo, lo + width)
    gate = jnp.dot(h, wg_ref[:, sl], preferred_element_type=F32)
    up = jnp.dot(h, wu_ref[:, sl], preferred_element_type=F32)
    act = ((gate * jax.nn.sigmoid(gate)) * up).astype(BF16)
    return y + jnp.dot(act, wd_ref[sl, :], preferred_element_type=F32)


FFN_ROW_STREAMS = 2


def _merge_ffn_kernel(x_ref, oa_ref, or_ref, wo_ref, g_ref, wg_ref, wu_ref, wd_ref, y_ref):
    rows = x_ref.shape[0] // FFN_ROW_STREAMS
    groups = [slice(s * rows, (s + 1) * rows) for s in range(FFN_ROW_STREAMS)]
    state = [_merge_rows(x_ref[g, :], oa_ref[g, :], or_ref[g, :], wo_ref, g_ref) for g in groups]
    lo = 0
    for width in _ff_chunks(wg_ref.shape[1]):
        state = [(_ffn_piece(h, y, wg_ref, wu_ref, wd_ref, lo, width), h) for y, h in state]
        lo += width
    for g, (y, _) in zip(groups, state):
        y_ref[g, :] = y


def _merge_ffn(x2d, o_a, o_r, prm, layer, tm=512):
    t_rows = x2d.shape[0]
    d_ff = prm["w_gate"].shape[-1]
    assert t_rows % tm == 0
    row = lambda i: (i, 0)
    return pl.pallas_call(
        _merge_ffn_kernel,
        grid=(t_rows // tm,),
        in_specs=[
            pl.BlockSpec((tm, D_MODEL), row),
            pl.BlockSpec((tm, WIDTH_A), row),
            pl.BlockSpec((tm, WIDTH_R), row),
            _layer_resident((D_MODEL, D_MODEL), layer),
            _layer_resident((1, D_MODEL), layer),
            _layer_resident((D_MODEL, d_ff), layer),
            _layer_resident((D_MODEL, d_ff), layer),
            _layer_resident((d_ff, D_MODEL), layer),
        ],
        out_specs=pl.BlockSpec((tm, D_MODEL), row),
        out_shape=jax.ShapeDtypeStruct((t_rows, D_MODEL), F32),
        compiler_params=_params(1),
        name="merge_ffn",
    )(x2d, o_a, o_r, prm["w_out"], prm["norm_ffn"], prm["w_gate"], prm["w_up"], prm["w_down"])


def _layer(x, prm, layer, bias, tail_rows, carried, cache_k=None, cache_v=None, state0=None):
    b, seq, _ = x.shape
    x2d = x.reshape(b * seq, D_MODEL)
    qa, ka, va, o_r, *carried = _proj(x2d, prm, layer, seq, tail_rows, carried, state0)
    r3 = lambda t: t.reshape(b, seq, SEG)
    o_a = _attention(r3(qa), r3(ka), r3(va), bias, layer, cache_k, cache_v)
    y = _merge_ffn(x2d, o_a.reshape(b * seq, WIDTH_A), o_r, prm, layer)
    return y.reshape(b, seq, D_MODEL), carried


def _tail_heads(t, b, tail_rows):
    depth = t.shape[0]
    if t.ndim == 4:
        return t.reshape(depth, b, HEADS_A, HEAD_DIM_A, tail_rows).transpose(0, 1, 4, 2, 3)
    return t.reshape(depth, b, tail_rows, HEADS_A, HEAD_DIM_A)


def kernel(x_prompt, x_sample, cache_a_k, cache_a_v, state_ret, norm_mix, w_in, q_norm, k_norm,
           rel_table, ret_norm, w_out, norm_ffn, w_gate, w_up, w_down):
    depth = w_in.shape[0]
    bp, sq_p, _ = x_prompt.shape
    bs, sq_s, _ = x_sample.shape
    assert cache_a_k.shape[2] == BAND_ROWS
    rows_p = min(BAND_ROWS, sq_p)

    vec = lambda t: t.reshape(depth, 1, -1)
    prm = dict(
        norm_mix=vec(norm_mix), norm_ffn=vec(norm_ffn), ret_norm=vec(ret_norm),
        q_norm=vec(jnp.tile(q_norm, (1, HEADS_A))), k_norm=vec(jnp.tile(k_norm, (1, HEADS_A))),
        w_in=w_in.astype(BF16), w_out=w_out.astype(BF16),
        w_gate=w_gate.astype(BF16), w_up=w_up.astype(BF16), w_down=w_down.astype(BF16))
    bias_p = _band_bias(rel_table, ATTN_CHUNKS_PER_TILE)
    bias_s = _band_bias(rel_table, 1)
    feature_major = lambda c: c.transpose(0, 1, 3, 4, 2).reshape(depth, bs, WIDTH_A, BAND_ROWS)
    cache_k, cache_v = feature_major(cache_a_k), feature_major(cache_a_v)

    xp, xs = x_prompt, x_sample
    out_p, outs_s = None, []
    for l in range(depth):
        xp, out_p = _layer(xp, prm, l, bias_p, rows_p, out_p)
        xs, out_s = _layer(xs, prm, l, bias_s, sq_s, None, cache_k, cache_v, state_ret)
        outs_s.append(out_s)
    out_s = [jnp.concatenate(parts, axis=0) for parts in zip(*outs_s)]
    return (xp, xs, _tail_heads(out_p[0], bp, rows_p), _tail_heads(out_p[1], bp, rows_p), out_p[2],
            _tail_heads(out_s[0], bs, sq_s), _tail_heads(out_s[1], bs, sq_s), out_s[2])
```

```python
import functools

import jax
import jax.numpy as jnp
from jax import lax
from jax.experimental import pallas as pl
from jax.experimental.pallas import tpu as pltpu

F32 = jnp.float32
BF16 = jnp.bfloat16

D_MODEL = 1024
CHUNK = 64
N_PREV_CHUNKS = 8
BAND_ROWS = N_PREV_CHUNKS * CHUNK
HEADS_A = 8
HEAD_DIM_A = 64
WIDTH_A = HEADS_A * HEAD_DIM_A
HEADS_R = 4
HEAD_DIM_R = 128
WIDTH_R = HEADS_R * HEAD_DIM_R
N_SEG = 7
SEG = 512
REL_CLIP = 2 * CHUNK
ROPE_BASE = 10000.0
RMS_EPS = 1e-6
GN_EPS = 1e-5
NEG_INF = -1e30
LOG2_E = 1.4426950408889634

VMEM_LIMIT_BYTES = 56 * 1024 * 1024
LANES = 128
MXU_DIM = 256


def _resident(shape):
    nd = len(shape)
    return pl.BlockSpec(shape, lambda *_: (0,) * nd, pipeline_mode=pl.Buffered(1))


def _layer_resident(shape, layer):
    nd = len(shape)
    return pl.BlockSpec((None,) + tuple(shape), lambda *_: (layer,) + (0,) * nd,
                        pipeline_mode=pl.Buffered(1))


def _params(n_axes):
    return pltpu.CompilerParams(dimension_semantics=("arbitrary",) * n_axes,
                                vmem_limit_bytes=VMEM_LIMIT_BYTES)


def _retention_block(q, k, v, g, s_prev, dec_ref, qd_ref, kd_ref, sd_ref, rn_ref):
    outs, s_new = [], []
    for hd in range(HEADS_R):
        sl = slice(hd * HEAD_DIM_R, (hd + 1) * HEAD_DIM_R)
        qh, kh, vh = q[:, sl], k[:, sl], v[:, sl]
        scores = lax.dot_general(qh, kh.astype(BF16), NT_DIMS, preferred_element_type=F32) * dec_ref[hd]
        intra = jnp.dot(scores.astype(BF16), vh, preferred_element_type=F32)
        cross = jnp.dot(qh, s_prev[hd].astype(BF16), preferred_element_type=F32) * qd_ref[:, sl]
        k_dec = (kh * kd_ref[:, sl]).astype(BF16)
        s_new.append(sd_ref[hd] * s_prev[hd] + lax.dot_general(
            k_dec, vh, (((0,), (0,)), ((), ())), preferred_element_type=F32))
        o = intra + cross
        mu = jnp.mean(o, axis=-1, keepdims=True)
        d = o - mu
        var = jnp.mean(d * d, axis=-1, keepdims=True)
        outs.append(d * lax.rsqrt(var + GN_EPS))
    o_norm = jnp.concatenate(outs, axis=-1) * rn_ref[...]
    return (g * jax.nn.sigmoid(g)) * o_norm, s_new


def _rebase_state(s, rc_ref, rs_ref):
    half = HEAD_DIM_R // 2
    s1, s2 = s[:half], s[half:]
    return jnp.concatenate([s1 * rc_ref[...] - s2 * rs_ref[...],
                            s1 * rs_ref[...] + s2 * rc_ref[...]], axis=0)


def _proj_kernel(*refs, slot, tail_period, tail_first, ret_rows, seq_tiles, has_state):
    (x_ref, g_ref, w_ref, qn_ref, kn_ref, cos_ref, sin_ref,
     dec_ref, qd_ref, kd_ref, sd_ref, rn_ref, rc_ref, rs_ref) = refs[:14]
    n_in = 14
    if has_state:
        s0_ref = refs[n_in]
        n_in += 1
    if slot > 0:
        kt_prev, vt_prev, sout_prev = refs[n_in:n_in + 3]
        n_in += 3
    qa_ref, ka_ref, va_ref, or_ref, kt_ref, vt_ref, sout_ref = refs[n_in:n_in + 7]
    if not has_state:
        tile_in_seq = pl.program_id(0) % seq_tiles

        @pl.when(tile_in_seq == 0)
        def _():
            refs[-1][...] = jnp.zeros(refs[-1].shape, F32)
    x = x_ref[...]
    ms = jnp.mean(x * x, axis=-1, keepdims=True)
    h = ((x * lax.rsqrt(ms + RMS_EPS)) * g_ref[...]).astype(BF16)

    def seg(j):
        return jnp.dot(h, w_ref[:, j * SEG:(j + 1) * SEG], preferred_element_type=F32)

    def head_rms(z, gain):
        z2 = z * z
        lane = lax.broadcasted_iota(jnp.int32, (z.shape[0], LANES), 1)
        low = lane < HEAD_DIM_A
        parts = []
        for c in range(SEG // LANES):
            v = z2[:, c * LANES:(c + 1) * LANES]
            s_lo = jnp.sum(jnp.where(low, v, 0.0), axis=-1, keepdims=True)
            s_hi = jnp.sum(jnp.where(low, 0.0, v), axis=-1, keepdims=True)
            parts.append(jnp.where(low, s_lo, s_hi))
        msq = jnp.concatenate(parts, axis=-1) * (1.0 / HEAD_DIM_A)
        return (z * lax.rsqrt(msq + RMS_EPS)) * gain

    def rotary(z):
        outs = []
        for hd in range(HEADS_R):
            zh = z[:, hd * HEAD_DIM_R:(hd + 1) * HEAD_DIM_R]
            outs.append(zh * cos_ref[...] + pltpu.roll(zh, HEAD_DIM_R // 2, axis=1) * sin_ref[...])
        return jnp.concatenate(outs, axis=-1)

    mixer_a = {}

    def project_qa():
        qa_ref[...] = (head_rms(seg(0), qn_ref[...]) * (HEAD_DIM_A ** -0.5 * LOG2_E)).astype(BF16)

    def project_ka():
        mixer_a["k"] = head_rms(seg(1), kn_ref[...])
        ka_ref[...] = mixer_a["k"].astype(BF16)

    def project_va():
        mixer_a["v"] = seg(2)
        va_ref[...] = mixer_a["v"].astype(BF16)

    pending = [project_qa, project_ka, project_va]

    qr = rotary(seg(3)).astype(BF16)
    kr = rotary(seg(4)) * (HEAD_DIM_R ** -0.5)
    vr = seg(5).astype(BF16)
    gr = seg(6)
    tables = (dec_ref, qd_ref, kd_ref, sd_ref, rn_ref)
    n_blocks = x.shape[0] // ret_rows
    if not has_state:
        state = refs[-1]
        s_cur = [state[hd] for hd in range(HEADS_R)]
    for j in range(n_blocks):
        rows = slice(j * ret_rows, (j + 1) * ret_rows)
        if has_state:
            s_cur = [s0_ref[j, hd] for hd in range(HEADS_R)]
        o, s_cur = _retention_block(qr[rows], kr[rows], vr[rows], gr[rows], s_cur, *tables)
        or_ref[rows, :] = o.astype(BF16)
        if has_state:
            for hd in range(HEADS_R):
                sout_ref[slot, j, hd] = _rebase_state(s_cur[hd], rc_ref, rs_ref)
        if pending:
            pending.pop(0)()
    for task in pending:
        task()
    if has_state:
        if slot > 0:
            sout_ref[0:slot] = sout_prev[...]
    else:
        for hd in range(HEADS_R):
            state[hd] = s_cur[hd]

        @pl.when(tile_in_seq == seq_tiles - 1)
        def _():
            if slot > 0:
                sout_ref[0:slot] = sout_prev[...]
            for hd in range(HEADS_R):
                sout_ref[slot, 0, hd] = _rebase_state(state[hd], rc_ref, rs_ref)

    ka, va = mixer_a["k"], mixer_a["v"]

    def write_tails():
        if slot > 0:
            kt_ref[0:slot] = kt_prev[...]
            vt_ref[0:slot] = vt_prev[...]
        if len(kt_ref.shape) == 4:
            kt_ref[slot, 0] = ka.T
            vt_ref[slot, 0] = va.T
            return
        for hd in range(HEADS_A):
            rows = pl.ds(hd, ka.shape[0], stride=HEADS_A)
            cols = slice(hd * HEAD_DIM_A, (hd + 1) * HEAD_DIM_A)
            kt_ref[slot, rows, :] = ka[:, cols]
            vt_ref[slot, rows, :] = va[:, cols]

    if tail_period == 1:
        write_tails()
    else:
        pl.when(pl.program_id(0) % tail_period >= tail_first)(write_tails)


RETENTION_BLOCK = 256


def _retention_tables(blk, seq):
    log_gamma = jnp.log(1.0 - 2.0 ** (-5.0 - jnp.arange(HEADS_R, dtype=F32)))
    idx = jnp.arange(blk, dtype=F32)
    diff = idx[:, None] - idx[None, :]
    decay = jnp.where(diff[None] >= 0,
                      jnp.exp(jnp.maximum(diff, 0.0)[None] * log_gamma[:, None, None]), 0.0)
    lanes = lambda t: jnp.repeat(t, HEAD_DIM_R, axis=-1)
    q_decay = lanes(jnp.exp((idx + 1.0)[:, None] * log_gamma[None, :]))
    k_decay = lanes(jnp.exp((blk - 1.0 - idx)[:, None] * log_gamma[None, :]))
    s_decay = jnp.broadcast_to(jnp.exp(blk * log_gamma)[:, None, None], (HEADS_R, 1, HEAD_DIM_R))
    half = HEAD_DIM_R // 2
    inv = ROPE_BASE ** (-jnp.arange(half, dtype=F32) / half)
    ang = jnp.asarray(-seq, dtype=F32) * inv
    reb_cos = jnp.broadcast_to(jnp.cos(ang)[:, None], (half, HEAD_DIM_R))
    reb_sin = jnp.broadcast_to(jnp.sin(ang)[:, None], (half, HEAD_DIM_R))
    return decay, q_decay, k_decay, s_decay, reb_cos, reb_sin


def _proj(x2d, prm, layer, seq, tail_rows, prev, state0=None, tm=512):
    t_rows = x2d.shape[0]
    assert t_rows % tm == 0
    n_tiles = t_rows // tm
    n_seq = t_rows // seq
    has_state = state0 is not None
    if seq >= tm:
        assert seq % tm == 0 and tail_rows % tm == 0 and not has_state
        tail_period = seq // tm
        tail_first = (seq - tail_rows) // tm
        table_rows = seq
        ret_rows = min(RETENTION_BLOCK, tm)
    else:
        assert tm % seq == 0 and tail_rows == seq and has_state
        tail_period, tail_first = 1, 0
        table_rows = tm
        ret_rows = seq
    assert tm % ret_rows == 0
    n_tab = table_rows // tm
    seqs_per_tile = max(1, tm // seq)
    decay, q_decay, k_decay, s_decay, reb_cos, reb_sin = _retention_tables(ret_rows, seq)

    half = HEAD_DIM_R // 2
    inv = ROPE_BASE ** (-jnp.arange(half, dtype=F32) / half)
    pos = (jnp.arange(table_rows) % seq).astype(F32)
    ang = pos[:, None] * inv
    cos, sin = jnp.cos(ang), jnp.sin(ang)
    cos_t = jnp.concatenate([cos, cos], axis=-1)
    sin_t = jnp.concatenate([-sin, sin], axis=-1)

    row = lambda i: (i, 0)
    out_bf = jax.ShapeDtypeStruct((t_rows, SEG), BF16)
    slot = 0 if prev is None else prev[0].shape[0]
    state_dims = (seqs_per_tile, HEADS_R, HEAD_DIM_R, HEAD_DIM_R)
    state_full = (n_seq, HEADS_R, HEAD_DIM_R, HEAD_DIM_R)
    state_map = lambda i: (0, i // tail_period, 0, 0, 0)
    if seq >= tm:
        tail_full, tail_dims = (n_seq, SEG, tail_rows), (1, SEG, tm)
        tail_map = lambda i: (0, i // tail_period, 0, jnp.maximum(i % tail_period - tail_first, 0))
    else:
        tail_full, tail_dims = (t_rows * HEADS_A, HEAD_DIM_A), (tm * HEADS_A, HEAD_DIM_A)
        tail_map = lambda i: (0, i, 0)
    stacked = lambda n, dims, index_map: pl.BlockSpec((n,) + dims, index_map)
    tail_blk = stacked(slot + 1, tail_dims, tail_map)
    state_blk = stacked(slot + 1, state_dims, state_map)
    tail_shape = jax.ShapeDtypeStruct((slot + 1,) + tail_full, F32)
    state_shape = jax.ShapeDtypeStruct((slot + 1,) + state_full, F32)
    blk = pl.BlockSpec((tm, SEG), row)
    in_specs = [
        pl.BlockSpec((tm, D_MODEL), row),
        _layer_resident((1, D_MODEL), layer),
        _layer_resident((D_MODEL, N_SEG * SEG), layer),
        _layer_resident((1, SEG), layer),
        _layer_resident((1, SEG), layer),
        pl.BlockSpec((tm, HEAD_DIM_R), lambda i: (i % n_tab, 0)),
        pl.BlockSpec((tm, HEAD_DIM_R), lambda i: (i % n_tab, 0)),
        _resident(decay.shape), _resident(q_decay.shape), _resident(k_decay.shape),
        _resident(s_decay.shape), _layer_resident((1, WIDTH_R), layer),
        _resident(reb_cos.shape), _resident(reb_sin.shape),
    ]
    args = [x2d, prm["norm_mix"], prm["w_in"], prm["q_norm"], prm["k_norm"], cos_t, sin_t,
            decay, q_decay, k_decay, s_decay, prm["ret_norm"], reb_cos, reb_sin]
    scratch = []
    if has_state:
        in_specs.append(pl.BlockSpec((None,) + state_dims, lambda i: (layer, i, 0, 0, 0)))
        args.append(state0)
    else:
        scratch.append(pltpu.VMEM((HEADS_R, HEAD_DIM_R, HEAD_DIM_R), F32))
    if slot > 0:
        in_specs += [stacked(slot, tail_dims, tail_map), stacked(slot, tail_dims, tail_map),
                     stacked(slot, state_dims, state_map)]
        args += list(prev)
    return pl.pallas_call(
        functools.partial(_proj_kernel, slot=slot, tail_period=tail_period, tail_first=tail_first,
                          ret_rows=ret_rows, seq_tiles=tail_period, has_state=has_state),
        grid=(n_tiles,),
        in_specs=in_specs,
        out_specs=[blk, blk, blk, blk, tail_blk, tail_blk, state_blk],
        out_shape=[out_bf, out_bf, out_bf, out_bf, tail_shape, tail_shape, state_shape],
        scratch_shapes=scratch,
        compiler_params=_params(1),
        name="proj",
    )(*args)


NT_DIMS = (((1,), (1,)), ((), ()))
ATTN_SCORES_AHEAD = 2


def _pair_scores(tile, pair, bias_ref):
    q_pair = tile["q"](pair)
    tq = q_pair.shape[0]
    lane = lax.broadcasted_iota(jnp.int32, (tq, LANES), 1)
    zero = jnp.zeros_like(q_pair)
    qz = jnp.concatenate([jnp.where(lane < HEAD_DIM_A, q_pair, zero),
                          jnp.where(lane >= HEAD_DIM_A, q_pair, zero)], axis=0)
    return tile["qk"](pair, qz) + bias_ref[pair, :, tile["lo"]:]


def _pair_output(tile, pair, s):
    tq = s.shape[0] // 2
    m = jnp.max(s, axis=-1, keepdims=True)
    p = jnp.exp2(s - m)
    denom = jnp.sum(p, axis=-1, keepdims=True)
    r = tile["pv"](pair, p.astype(BF16)) / denom
    lane = lax.broadcasted_iota(jnp.int32, (tq, LANES), 1)
    return jnp.where(lane < HEAD_DIM_A, r[:tq], r[tq:])


def _attention_units(tiles, bias_ref):
    n_pairs = HEADS_A // 2
    units = [(tile, pair) for tile in tiles for pair in range(n_pairs)]
    ahead = [_pair_scores(*unit, bias_ref) for unit in units[:ATTN_SCORES_AHEAD]]
    yield
    outs = []
    for idx, (tile, pair) in enumerate(units):
        s_cur = ahead.pop(0)
        if idx + ATTN_SCORES_AHEAD < len(units):
            ahead.append(_pair_scores(*units[idx + ATTN_SCORES_AHEAD], bias_ref))
        outs.append(_pair_output(tile, pair, s_cur))
        if pair == n_pairs - 1:
            tile["store"](jnp.concatenate(outs, axis=-1).astype(BF16))
            outs = []
        yield


def _run_attention_tiles(tiles, bias_ref):
    for _ in _attention_units(tiles, bias_ref):
        pass


def _attn_prompt_kernel(q_ref, k_ref, v_ref, bias_ref, o_ref, *, tq, tps, n_steps):
    nk = bias_ref.shape[-1]
    n_special = BAND_ROWS // tq
    step = pl.program_id(1)
    lanes = lambda pair: pl.ds(pair * LANES, LANES)

    def make_tile(i, row0, nkv):
        rows = pl.ds(row0, nkv)
        qrows = pl.ds(i * tq, tq)

        def store(o):
            o_ref[0, qrows, :] = o

        def qk(pair, qz):
            return lax.dot_general(qz, k_ref[0, rows, lanes(pair)], NT_DIMS, preferred_element_type=F32)

        def pv(pair, p):
            return jnp.dot(p, v_ref[0, rows, lanes(pair)], preferred_element_type=F32)
        return dict(q=lambda pair: q_ref[0, qrows, lanes(pair)], qk=qk, pv=pv, lo=nk - nkv, store=store)

    for j in range(min(n_special // tps, n_steps)):
        @pl.when(step == j)
        def _(j=j):
            _run_attention_tiles(
                [make_tile(i, 0, (j * tps + i + 1) * tq) for i in range(tps)], bias_ref)

    if n_steps > n_special // tps:
        @pl.when(step >= n_special // tps)
        def _():
            tiles = []
            for i in range(tps):
                row0 = pl.multiple_of((step * tps + i) * tq - BAND_ROWS, tq)
                tiles.append(make_tile(i, row0, nk))
            _run_attention_tiles(tiles, bias_ref)


def _attn_cache_kernel(q_ref, k_ref, v_ref, bias_ref, ck_ref, cv_ref, o_ref, *, bb):
    lanes = lambda pair: pl.ds(pair * LANES, LANES)

    def make_tile(bi):
        past = lambda ref, pair: ref[bi, lanes(pair), :].astype(BF16)

        def store(o):
            o_ref[bi] = o

        def qk(pair, qz):
            s_past = jnp.dot(qz, past(ck_ref, pair), preferred_element_type=F32)
            s_new = lax.dot_general(qz, k_ref[bi, :, lanes(pair)], NT_DIMS, preferred_element_type=F32)
            return jnp.concatenate([s_past, s_new], axis=-1)

        def pv(pair, p):
            return (lax.dot_general(p[:, :BAND_ROWS], past(cv_ref, pair), NT_DIMS, preferred_element_type=F32)
                    + jnp.dot(p[:, BAND_ROWS:], v_ref[bi, :, lanes(pair)], preferred_element_type=F32))
        return dict(q=lambda pair: q_ref[bi, :, lanes(pair)], qk=qk, pv=pv, lo=0, store=store)

    _run_attention_tiles([make_tile(bi) for bi in range(bb)], bias_ref)


def _band_bias(rel_table, chunks_per_tile):
    n_band = (N_PREV_CHUNKS + 1) * CHUNK
    table = rel_table.astype(F32)
    lead = table.shape[:-1]
    u_min = BAND_ROWS + REL_CLIP - (n_band - 1)
    n_far = CHUNK - 1 + BAND_ROWS - REL_CLIP
    g = jnp.concatenate([table[..., u_min:], jnp.broadcast_to(table[..., -1:], lead + (n_far,))], axis=-1)
    r = jnp.concatenate([g[..., ::-1], jnp.zeros(lead + (1,), F32)], axis=-1)
    period = r.shape[-1]
    tiled = jnp.tile(r, (1,) * len(lead) + (CHUNK + 1,))[..., :CHUNK * (period + 1)]
    hankel = tiled.reshape(lead + (CHUNK, period + 1))[..., :n_band]
    base = hankel[..., ::-1, :]
    pad_lead = ((0, 0),) * (len(lead) + 1)
    rows = [jnp.pad(base, pad_lead + ((i * CHUNK, (chunks_per_tile - 1 - i) * CHUNK),),
                    constant_values=NEG_INF) for i in range(chunks_per_tile)]
    bias = jnp.concatenate(rows, axis=-2) * LOG2_E
    return bias.reshape(lead[:-1] + (HEADS_A // 2, 2 * chunks_per_tile * CHUNK, bias.shape[-1]))


ATTN_CHUNKS_PER_TILE = 2
ATTN_TILES_PER_STEP = 4
ATTN_CACHE_BATCH = 4


def _attention(qa, ka, va, bias, layer, cache_k=None, cache_v=None):
    b, seq, _ = qa.shape
    out_shape = jax.ShapeDtypeStruct((b, seq, WIDTH_A), BF16)
    bias_spec = _layer_resident(bias.shape[1:], layer)
    if cache_k is None:
        tps = ATTN_TILES_PER_STEP
        tq = ATTN_CHUNKS_PER_TILE * CHUNK
        assert seq % (tq * tps) == 0 and (BAND_ROWS // tq) % tps == 0
        n_steps = seq // (tq * tps)
        tile = pl.BlockSpec((1, tq * tps, WIDTH_A), lambda i, t: (i, t, 0))
        whole = pl.BlockSpec((1, seq, WIDTH_A), lambda i, t: (i, 0, 0))
        return pl.pallas_call(
            functools.partial(_attn_prompt_kernel, tq=tq, tps=tps, n_steps=n_steps),
            grid=(b, n_steps),
            in_specs=[tile, whole, whole, bias_spec],
            out_specs=tile,
            out_shape=out_shape,
            compiler_params=_params(2),
            name="band_attention",
        )(qa, ka, va, bias)
    assert seq == CHUNK
    bb = ATTN_CACHE_BATCH
    assert b % bb == 0
    new = pl.BlockSpec((bb, seq, WIDTH_A), lambda i: (i, 0, 0))
    cache = pl.BlockSpec((None, bb, WIDTH_A, BAND_ROWS), lambda i: (layer, i, 0, 0))
    return pl.pallas_call(
        functools.partial(_attn_cache_kernel, bb=bb),
        grid=(b // bb,),
        in_specs=[new, new, new, bias_spec, cache, cache],
        out_specs=new,
        out_shape=out_shape,
        compiler_params=_params(1),
        name="band_attention_cache",
    )(qa, ka, va, bias, cache_k, cache_v)


def _ff_chunks(d_ff):
    n_tiles = d_ff // MXU_DIM
    first = (n_tiles + 1) // 2 * MXU_DIM
    return [c for c in (first, d_ff - first) if c > 0] if n_tiles > 1 else [d_ff]


def _merge_rows(x, o_a, o_r, wo_ref, g_ref):
    mix = jnp.concatenate([o_a, o_r], axis=-1)
    x = x + jnp.dot(mix, wo_ref[...], preferred_element_type=F32)
    ms = jnp.mean(x * x, axis=-1, keepdims=True)
    return x, ((x * lax.rsqrt(ms + RMS_EPS)) * g_ref[...]).astype(BF16)


def _ffn_piece(h, y, wg_ref, wu_ref, wd_ref, lo, width):
    sl = slice(lo, lo + width)
    gate = jnp.dot(h, wg_ref[:, sl], preferred_element_type=F32)
    up = jnp.dot(h, wu_ref[:, sl], preferred_element_type=F32)
    act = ((gate * jax.nn.sigmoid(gate)) * up).astype(BF16)
    return y + jnp.dot(act, wd_ref[sl, :], preferred_element_type=F32)


FFN_ROW_STREAMS = 2


def _merge_ffn_kernel(x_ref, oa_ref, or_ref, wo_ref, g_ref, wg_ref, wu_ref, wd_ref, y_ref):
    rows = x_ref.shape[0] // FFN_ROW_STREAMS
    groups = [slice(s * rows, (s + 1) * rows) for s in range(FFN_ROW_STREAMS)]
    state = [_merge_rows(x_ref[g, :], oa_ref[g, :], or_ref[g, :], wo_ref, g_ref) for g in groups]
    lo = 0
    for width in _ff_chunks(wg_ref.shape[1]):
        state = [(_ffn_piece(h, y, wg_ref, wu_ref, wd_ref, lo, width), h) for y, h in state]
        lo += width
    for g, (y, _) in zip(groups, state):
        y_ref[g, :] = y


def _merge_ffn(x2d, o_a, o_r, prm, layer, tm=512):
    t_rows = x2d.shape[0]
    d_ff = prm["w_gate"].shape[-1]
    assert t_rows % tm == 0
    row = lambda i: (i, 0)
    return pl.pallas_call(
        _merge_ffn_kernel,
        grid=(t_rows // tm,),
        in_specs=[
            pl.BlockSpec((tm, D_MODEL), row),
            pl.BlockSpec((tm, WIDTH_A), row),
            pl.BlockSpec((tm, WIDTH_R), row),
            _layer_resident((D_MODEL, D_MODEL), layer),
            _layer_resident((1, D_MODEL), layer),
            _layer_resident((D_MODEL, d_ff), layer),
            _layer_resident((D_MODEL, d_ff), layer),
            _layer_resident((d_ff, D_MODEL), layer),
        ],
        out_specs=pl.BlockSpec((tm, D_MODEL), row),
        out_shape=jax.ShapeDtypeStruct((t_rows, D_MODEL), F32),
        compiler_params=_params(1),
        name="merge_ffn",
    )(x2d, o_a, o_r, prm["w_out"], prm["norm_ffn"], prm["w_gate"], prm["w_up"], prm["w_down"])


def _layer(x, prm, layer, bias, tail_rows, carried, cache_k=None, cache_v=None, state0=None):
    b, seq, _ = x.shape
    x2d = x.reshape(b * seq, D_MODEL)
    qa, ka, va, o_r, *carried = _proj(x2d, prm, layer, seq, tail_rows, carried, state0)
    r3 = lambda t: t.reshape(b, seq, SEG)
    o_a = _attention(r3(qa), r3(ka), r3(va), bias, layer, cache_k, cache_v)
    y = _merge_ffn(x2d, o_a.reshape(b * seq, WIDTH_A), o_r, prm, layer)
    return y.reshape(b, seq, D_MODEL), carried


def _tail_heads(t, b, tail_rows):
    depth = t.shape[0]
    if t.ndim == 4:
        return t.reshape(depth, b, HEADS_A, HEAD_DIM_A, tail_rows).transpose(0, 1, 4, 2, 3)
    return t.reshape(depth, b, tail_rows, HEADS_A, HEAD_DIM_A)


def kernel(x_prompt, x_sample, cache_a_k, cache_a_v, state_ret, norm_mix, w_in, q_norm, k_norm,
           rel_table, ret_norm, w_out, norm_ffn, w_gate, w_up, w_down):
    depth = w_in.shape[0]
    bp, sq_p, _ = x_prompt.shape
    bs, sq_s, _ = x_sample.shape
    assert cache_a_k.shape[2] == BAND_ROWS
    rows_p = min(BAND_ROWS, sq_p)

    vec = lambda t: t.reshape(depth, 1, -1)
    prm = dict(
        norm_mix=vec(norm_mix), norm_ffn=vec(norm_ffn), ret_norm=vec(ret_norm),
        q_norm=vec(jnp.tile(q_norm, (1, HEADS_A))), k_norm=vec(jnp.tile(k_norm, (1, HEADS_A))),
        w_in=w_in.astype(BF16), w_out=w_out.astype(BF16),
        w_gate=w_gate.astype(BF16), w_up=w_up.astype(BF16), w_down=w_down.astype(BF16))
    bias_p = _band_bias(rel_table, ATTN_CHUNKS_PER_TILE)
    bias_s = _band_bias(rel_table, 1)
    feature_major = lambda c: c.transpose(0, 1, 3, 4, 2).reshape(depth, bs, WIDTH_A, BAND_ROWS)
    cache_k, cache_v = feature_major(cache_a_k), feature_major(cache_a_v)

    xp, xs = x_prompt, x_sample
    out_p, outs_s = None, []
    for l in range(depth):
        xp, out_p = _layer(xp, prm, l, bias_p, rows_p, out_p)
        xs, out_s = _layer(xs, prm, l, bias_s, sq_s, None, cache_k, cache_v, state_ret)
        outs_s.append(out_s)
    out_s = [jnp.concatenate(parts, axis=0) for parts in zip(*outs_s)]
    return (xp, xs, _tail_heads(out_p[0], bp, rows_p), _tail_heads(out_p[1], bp, rows_p), out_p[2],
            _tail_heads(out_s[0], bs, sq_s), _tail_heads(out_s[1], bs, sq_s), out_s[2])
```

```python
import functools

import jax
import jax.numpy as jnp
from jax import lax
from jax.experimental import pallas as pl
from jax.experimental.pallas import tpu as pltpu

F32 = jnp.float32
BF16 = jnp.bfloat16

D_MODEL = 1024
CHUNK = 64
N_PREV_CHUNKS = 8
BAND_ROWS = N_PREV_CHUNKS * CHUNK
HEADS_A = 8
HEAD_DIM_A = 64
WIDTH_A = HEADS_A * HEAD_DIM_A
HEADS_R = 4
HEAD_DIM_R = 128
WIDTH_R = HEADS_R * HEAD_DIM_R
N_SEG = 7
SEG = 512
REL_CLIP = 2 * CHUNK
ROPE_BASE = 10000.0
RMS_EPS = 1e-6
GN_EPS = 1e-5
NEG_INF = -1e30
LOG2_E = 1.4426950408889634

VMEM_LIMIT_BYTES = 56 * 1024 * 1024
LANES = 128
MXU_DIM = 256


def _resident(shape):
    nd = len(shape)
    return pl.BlockSpec(shape, lambda *_: (0,) * nd, pipeline_mode=pl.Buffered(1))


def _layer_resident(shape, layer):
    nd = len(shape)
    return pl.BlockSpec((None,) + tuple(shape), lambda *_: (layer,) + (0,) * nd,
                        pipeline_mode=pl.Buffered(1))


def _params(n_axes):
    return pltpu.CompilerParams(dimension_semantics=("arbitrary",) * n_axes,
                                vmem_limit_bytes=VMEM_LIMIT_BYTES)


def _retention_block(q, k, v, g, s_prev, dec_ref, qd_ref, kd_ref, sd_ref, rn_ref):
    outs, s_new = [], []
    for hd in range(HEADS_R):
        sl = slice(hd * HEAD_DIM_R, (hd + 1) * HEAD_DIM_R)
        qh, kh, vh = q[:, sl], k[:, sl], v[:, sl]
        scores = lax.dot_general(qh, kh.astype(BF16), NT_DIMS, preferred_element_type=F32) * dec_ref[hd]
        intra = jnp.dot(scores.astype(BF16), vh, preferred_element_type=F32)
        cross = jnp.dot(qh, s_prev[hd].astype(BF16), preferred_element_type=F32) * qd_ref[:, sl]
        k_dec = (kh * kd_ref[:, sl]).astype(BF16)
        s_new.append(sd_ref[hd] * s_prev[hd] + lax.dot_general(
            k_dec, vh, (((0,), (0,)), ((), ())), preferred_element_type=F32))
        o = intra + cross
        mu = jnp.mean(o, axis=-1, keepdims=True)
        d = o - mu
        var = jnp.mean(d * d, axis=-1, keepdims=True)
        outs.append(d * lax.rsqrt(var + GN_EPS))
    o_norm = jnp.concatenate(outs, axis=-1) * rn_ref[...]
    return (g * jax.nn.sigmoid(g)) * o_norm, s_new


def _rebase_state(s, rc_ref, rs_ref):
    half = HEAD_DIM_R // 2
    s1, s2 = s[:half], s[half:]
    return jnp.concatenate([s1 * rc_ref[...] - s2 * rs_ref[...],
                            s1 * rs_ref[...] + s2 * rc_ref[...]], axis=0)


def _proj_kernel(*refs, slot, tail_period, tail_first, ret_rows, seq_tiles, has_state):
    (x_ref, g_ref, w_ref, qn_ref, kn_ref, cos_ref, sin_ref,
     dec_ref, qd_ref, kd_ref, sd_ref, rn_ref, rc_ref, rs_ref) = refs[:14]
    n_in = 14
    if has_state:
        s0_ref = refs[n_in]
        n_in += 1
    if slot > 0:
        kt_prev, vt_prev, sout_prev = refs[n_in:n_in + 3]
        n_in += 3
    qa_ref, ka_ref, va_ref, or_ref, kt_ref, vt_ref, sout_ref = refs[n_in:n_in + 7]
    if not has_state:
        tile_in_seq = pl.program_id(0) % seq_tiles

        @pl.when(tile_in_seq == 0)
        def _():
            refs[-1][...] = jnp.zeros(refs[-1].shape, F32)
    x = x_ref[...]
    ms = jnp.mean(x * x, axis=-1, keepdims=True)
    h = ((x * lax.rsqrt(ms + RMS_EPS)) * g_ref[...]).astype(BF16)

    def seg(j):
        return jnp.dot(h, w_ref[:, j * SEG:(j + 1) * SEG], preferred_element_type=F32)

    def head_rms(z, gain):
        z2 = z * z
        lane = lax.broadcasted_iota(jnp.int32, (z.shape[0], LANES), 1)
        low = lane < HEAD_DIM_A
        parts = []
        for c in range(SEG // LANES):
            v = z2[:, c * LANES:(c + 1) * LANES]
            s_lo = jnp.sum(jnp.where(low, v, 0.0), axis=-1, keepdims=True)
            s_hi = jnp.sum(jnp.where(low, 0.0, v), axis=-1, keepdims=True)
            parts.append(jnp.where(low, s_lo, s_hi))
        msq = jnp.concatenate(parts, axis=-1) * (1.0 / HEAD_DIM_A)
        return (z * lax.rsqrt(msq + RMS_EPS)) * gain

    def rotary(z):
        outs = []
        for hd in range(HEADS_R):
            zh = z[:, hd * HEAD_DIM_R:(hd + 1) * HEAD_DIM_R]
            outs.append(zh * cos_ref[...] + pltpu.roll(zh, HEAD_DIM_R // 2, axis=1) * sin_ref[...])
        return jnp.concatenate(outs, axis=-1)

    mixer_a = {}

    def project_qa():
        qa_ref[...] = (head_rms(seg(0), qn_ref[...]) * (HEAD_DIM_A ** -0.5 * LOG2_E)).astype(BF16)

    def project_ka():
        mixer_a["k"] = head_rms(seg(1), kn_ref[...])
        ka_ref[...] = mixer_a["k"].astype(BF16)

    def project_va():
        mixer_a["v"] = seg(2)
        va_ref[...] = mixer_a["v"].astype(BF16)

    pending = [project_qa, project_ka, project_va]

    qr = rotary(seg(3)).astype(BF16)
    kr = rotary(seg(4)) * (HEAD_DIM_R ** -0.5)
    vr = seg(5).astype(BF16)
    gr = seg(6)
    tables = (dec_ref, qd_ref, kd_ref, sd_ref, rn_ref)
    n_blocks = x.shape[0] // ret_rows
    if not has_state:
        state = refs[-1]
        s_cur = [state[hd] for hd in range(HEADS_R)]
    for j in range(n_blocks):
        rows = slice(j * ret_rows, (j + 1) * ret_rows)
        if has_state:
            s_cur = [s0_ref[j, hd] for hd in range(HEADS_R)]
        o, s_cur = _retention_block(qr[rows], kr[rows], vr[rows], gr[rows], s_cur, *tables)
        or_ref[rows, :] = o.astype(BF16)
        if has_state:
            for hd in range(HEADS_R):
                sout_ref[slot, j, hd] = _rebase_state(s_cur[hd], rc_ref, rs_ref)
        if pending:
            pending.pop(0)()
    for task in pending:
        task()
    if has_state:
        if slot > 0:
            sout_ref[0:slot] = sout_prev[...]
    else:
        for hd in range(HEADS_R):
            state[hd] = s_cur[hd]

        @pl.when(tile_in_seq == seq_tiles - 1)
        def _():
            if slot > 0:
                sout_ref[0:slot] = sout_prev[...]
            for hd in range(HEADS_R):
                sout_ref[slot, 0, hd] = _rebase_state(state[hd], rc_ref, rs_ref)

    ka, va = mixer_a["k"], mixer_a["v"]

    def write_tails():
        if slot > 0:
            kt_ref[0:slot] = kt_prev[...]
            vt_ref[0:slot] = vt_prev[...]
        if len(kt_ref.shape) == 4:
            kt_ref[slot, 0] = ka.T
            vt_ref[slot, 0] = va.T
            return
        for hd in range(HEADS_A):
            rows = pl.ds(hd, ka.shape[0], stride=HEADS_A)
            cols = slice(hd * HEAD_DIM_A, (hd + 1) * HEAD_DIM_A)
            kt_ref[slot, rows, :] = ka[:, cols]
            vt_ref[slot, rows, :] = va[:, cols]

    if tail_period == 1:
        write_tails()
    else:
        pl.when(pl.program_id(0) % tail_period >= tail_first)(write_tails)


RETENTION_BLOCK = 256


def _retention_tables(blk, seq):
    log_gamma = jnp.log(1.0 - 2.0 ** (-5.0 - jnp.arange(HEADS_R, dtype=F32)))
    idx = jnp.arange(blk, dtype=F32)
    diff = idx[:, None] - idx[None, :]
    decay = jnp.where(diff[None] >= 0,
                      jnp.exp(jnp.maximum(diff, 0.0)[None] * log_gamma[:, None, None]), 0.0)
    lanes = lambda t: jnp.repeat(t, HEAD_DIM_R, axis=-1)
    q_decay = lanes(jnp.exp((idx + 1.0)[:, None] * log_gamma[None, :]))
    k_decay = lanes(jnp.exp((blk - 1.0 - idx)[:, None] * log_gamma[None, :]))
    s_decay = jnp.broadcast_to(jnp.exp(blk * log_gamma)[:, None, None], (HEADS_R, 1, HEAD_DIM_R))
    half = HEAD_DIM_R // 2
    inv = ROPE_BASE ** (-jnp.arange(half, dtype=F32) / half)
    ang = jnp.asarray(-seq, dtype=F32) * inv
    reb_cos = jnp.broadcast_to(jnp.cos(ang)[:, None], (half, HEAD_DIM_R))
    reb_sin = jnp.broadcast_to(jnp.sin(ang)[:, None], (half, HEAD_DIM_R))
    return decay, q_decay, k_decay, s_decay, reb_cos, reb_sin


def _proj(x2d, prm, layer, seq, tail_rows, prev, state0=None):
    t_rows = x2d.shape[0]
    tm = 512 if seq >= 512 else 256
    assert t_rows % tm == 0
    n_tiles = t_rows // tm
    n_seq = t_rows // seq
    has_state = state0 is not None
    if seq >= tm:
        assert seq % tm == 0 and tail_rows % tm == 0 and not has_state
        tail_period = seq // tm
        tail_first = (seq - tail_rows) // tm
        table_rows = seq
        ret_rows = min(RETENTION_BLOCK, tm)
    else:
        assert tm % seq == 0 and tail_rows == seq and has_state
        tail_period, tail_first = 1, 0
        table_rows = tm
        ret_rows = seq
    assert tm % ret_rows == 0
    n_tab = table_rows // tm
    seqs_per_tile = max(1, tm // seq)
    decay, q_decay, k_decay, s_decay, reb_cos, reb_sin = _retention_tables(ret_rows, seq)

    half = HEAD_DIM_R // 2
    inv = ROPE_BASE ** (-jnp.arange(half, dtype=F32) / half)
    pos = (jnp.arange(table_rows) % seq).astype(F32)
    ang = pos[:, None] * inv
    cos, sin = jnp.cos(ang), jnp.sin(ang)
    cos_t = jnp.concatenate([cos, cos], axis=-1)
    sin_t = jnp.concatenate([-sin, sin], axis=-1)

    row = lambda i: (i, 0)
    out_bf = jax.ShapeDtypeStruct((t_rows, SEG), BF16)
    slot = 0 if prev is None else prev[0].shape[0]
    state_dims = (seqs_per_tile, HEADS_R, HEAD_DIM_R, HEAD_DIM_R)
    state_full = (n_seq, HEADS_R, HEAD_DIM_R, HEAD_DIM_R)
    state_map = lambda i: (0, i // tail_period, 0, 0, 0)
    if seq >= tm:
        tail_full, tail_dims = (n_seq, SEG, tail_rows), (1, SEG, tm)
        tail_map = lambda i: (0, i // tail_period, 0, jnp.maximum(i % tail_period - tail_first, 0))
    else:
        tail_full, tail_dims = (t_rows * HEADS_A, HEAD_DIM_A), (tm * HEADS_A, HEAD_DIM_A)
        tail_map = lambda i: (0, i, 0)
    stacked = lambda n, dims, index_map: pl.BlockSpec((n,) + dims, index_map)
    tail_blk = stacked(slot + 1, tail_dims, tail_map)
    state_blk = stacked(slot + 1, state_dims, state_map)
    tail_shape = jax.ShapeDtypeStruct((slot + 1,) + tail_full, F32)
    state_shape = jax.ShapeDtypeStruct((slot + 1,) + state_full, F32)
    blk = pl.BlockSpec((tm, SEG), row)
    in_specs = [
        pl.BlockSpec((tm, D_MODEL), row),
        _layer_resident((1, D_MODEL), layer),
        _layer_resident((D_MODEL, N_SEG * SEG), layer),
        _layer_resident((1, SEG), layer),
        _layer_resident((1, SEG), layer),
        pl.BlockSpec((tm, HEAD_DIM_R), lambda i: (i % n_tab, 0)),
        pl.BlockSpec((tm, HEAD_DIM_R), lambda i: (i % n_tab, 0)),
        _resident(decay.shape), _resident(q_decay.shape), _resident(k_decay.shape),
        _resident(s_decay.shape), _layer_resident((1, WIDTH_R), layer),
        _resident(reb_cos.shape), _resident(reb_sin.shape),
    ]
    args = [x2d, prm["norm_mix"], prm["w_in"], prm["q_norm"], prm["k_norm"], cos_t, sin_t,
            decay, q_decay, k_decay, s_decay, prm["ret_norm"], reb_cos, reb_sin]
    scratch = []
    if has_state:
        in_specs.append(pl.BlockSpec((None,) + state_dims, lambda i: (layer, i, 0, 0, 0)))
        args.append(state0)
    else:
        scratch.append(pltpu.VMEM((HEADS_R, HEAD_DIM_R, HEAD_DIM_R), F32))
    if slot > 0:
        in_specs += [stacked(slot, tail_dims, tail_map), stacked(slot, tail_dims, tail_map),
                     stacked(slot, state_dims, state_map)]
        args += list(prev)
    return pl.pallas_call(
        functools.partial(_proj_kernel, slot=slot, tail_period=tail_period, tail_first=tail_first,
                          ret_rows=ret_rows, seq_tiles=tail_period, has_state=has_state),
        grid=(n_tiles,),
        in_specs=in_specs,
        out_specs=[blk, blk, blk, blk, tail_blk, tail_blk, state_blk],
        out_shape=[out_bf, out_bf, out_bf, out_bf, tail_shape, tail_shape, state_shape],
        scratch_shapes=scratch,
        compiler_params=_params(1),
        name="proj",
    )(*args)


NT_DIMS = (((1,), (1,)), ((), ()))
ATTN_SCORES_AHEAD = 2


def _pair_scores(tile, pair, bias_ref):
    q_pair = tile["q"](pair)
    tq = q_pair.shape[0]
    lane = lax.broadcasted_iota(jnp.int32, (tq, LANES), 1)
    zero = jnp.zeros_like(q_pair)
    qz = jnp.concatenate([jnp.where(lane < HEAD_DIM_A, q_pair, zero),
                          jnp.where(lane >= HEAD_DIM_A, q_pair, zero)], axis=0)
    return tile["qk"](pair, qz) + bias_ref[pair, :, tile["lo"]:]


def _pair_output(tile, pair, s):
    tq = s.shape[0] // 2
    m = jnp.max(s, axis=-1, keepdims=True)
    p = jnp.exp2(s - m)
    denom = jnp.sum(p, axis=-1, keepdims=True)
    r = tile["pv"](pair, p.astype(BF16)) / denom
    lane = lax.broadcasted_iota(jnp.int32, (tq, LANES), 1)
    return jnp.where(lane < HEAD_DIM_A, r[:tq], r[tq:])


def _attention_units(tiles, bias_ref):
    n_pairs = HEADS_A // 2
    units = [(tile, pair) for tile in tiles for pair in range(n_pairs)]
    ahead = [_pair_scores(*unit, bias_ref) for unit in units[:ATTN_SCORES_AHEAD]]
    yield
    outs = []
    for idx, (tile, pair) in enumerate(units):
        s_cur = ahead.pop(0)
        if idx + ATTN_SCORES_AHEAD < len(units):
            ahead.append(_pair_scores(*units[idx + ATTN_SCORES_AHEAD], bias_ref))
        outs.append(_pair_output(tile, pair, s_cur))
        if pair == n_pairs - 1:
            tile["store"](jnp.concatenate(outs, axis=-1).astype(BF16))
            outs = []
        yield


def _run_attention_tiles(tiles, bias_ref):
    for _ in _attention_units(tiles, bias_ref):
        pass


def _attn_prompt_kernel(q_ref, k_ref, v_ref, bias_ref, o_ref, *, tq, tps, n_steps):
    nk = bias_ref.shape[-1]
    n_special = BAND_ROWS // tq
    step = pl.program_id(1)
    lanes = lambda pair: pl.ds(pair * LANES, LANES)

    def make_tile(i, row0, nkv):
        rows = pl.ds(row0, nkv)
        qrows = pl.ds(i * tq, tq)

        def store(o):
            o_ref[0, qrows, :] = o

        def qk(pair, qz):
            return lax.dot_general(qz, k_ref[0, rows, lanes(pair)], NT_DIMS, preferred_element_type=F32)

        def pv(pair, p):
            return jnp.dot(p, v_ref[0, rows, lanes(pair)], preferred_element_type=F32)
        return dict(q=lambda pair: q_ref[0, qrows, lanes(pair)], qk=qk, pv=pv, lo=nk - nkv, store=store)

    for j in range(min(n_special // tps, n_steps)):
        @pl.when(step == j)
        def _(j=j):
            _run_attention_tiles(
                [make_tile(i, 0, (j * tps + i + 1) * tq) for i in range(tps)], bias_ref)

    if n_steps > n_special // tps:
        @pl.when(step >= n_special // tps)
        def _():
            tiles = []
            for i in range(tps):
                row0 = pl.multiple_of((step * tps + i) * tq - BAND_ROWS, tq)
                tiles.append(make_tile(i, row0, nk))
            _run_attention_tiles(tiles, bias_ref)


def _attn_cache_kernel(q_ref, k_ref, v_ref, bias_ref, ck_ref, cv_ref, o_ref, *, bb):
    lanes = lambda pair: pl.ds(pair * LANES, LANES)

    def make_tile(bi):
        past = lambda ref, pair: ref[bi, lanes(pair), :].astype(BF16)

        def store(o):
            o_ref[bi] = o

        def qk(pair, qz):
            s_past = jnp.dot(qz, past(ck_ref, pair), preferred_element_type=F32)
            s_new = lax.dot_general(qz, k_ref[bi, :, lanes(pair)], NT_DIMS, preferred_element_type=F32)
            return jnp.concatenate([s_past, s_new], axis=-1)

        def pv(pair, p):
            return (lax.dot_general(p[:, :BAND_ROWS], past(cv_ref, pair), NT_DIMS, preferred_element_type=F32)
                    + jnp.dot(p[:, BAND_ROWS:], v_ref[bi, :, lanes(pair)], preferred_element_type=F32))
        return dict(q=lambda pair: q_ref[bi, :, lanes(pair)], qk=qk, pv=pv, lo=0, store=store)

    _run_attention_tiles([make_tile(bi) for bi in range(bb)], bias_ref)


def _band_bias(rel_table, chunks_per_tile):
    n_band = (N_PREV_CHUNKS + 1) * CHUNK
    table = rel_table.astype(F32)
    lead = table.shape[:-1]
    u_min = BAND_ROWS + REL_CLIP - (n_band - 1)
    n_far = CHUNK - 1 + BAND_ROWS - REL_CLIP
    g = jnp.concatenate([table[..., u_min:], jnp.broadcast_to(table[..., -1:], lead + (n_far,))], axis=-1)
    r = jnp.concatenate([g[..., ::-1], jnp.zeros(lead + (1,), F32)], axis=-1)
    period = r.shape[-1]
    tiled = jnp.tile(r, (1,) * len(lead) + (CHUNK + 1,))[..., :CHUNK * (period + 1)]
    hankel = tiled.reshape(lead + (CHUNK, period + 1))[..., :n_band]
    base = hankel[..., ::-1, :]
    pad_lead = ((0, 0),) * (len(lead) + 1)
    rows = [jnp.pad(base, pad_lead + ((i * CHUNK, (chunks_per_tile - 1 - i) * CHUNK),),
                    constant_values=NEG_INF) for i in range(chunks_per_tile)]
    bias = jnp.concatenate(rows, axis=-2) * LOG2_E
    return bias.reshape(lead[:-1] + (HEADS_A // 2, 2 * chunks_per_tile * CHUNK, bias.shape[-1]))


ATTN_CHUNKS_PER_TILE = 2
ATTN_TILES_PER_STEP = 4
ATTN_CACHE_BATCH = 4


def _attention(qa, ka, va, bias, layer, cache_k=None, cache_v=None):
    b, seq, _ = qa.shape
    out_shape = jax.ShapeDtypeStruct((b, seq, WIDTH_A), BF16)
    bias_spec = _layer_resident(bias.shape[1:], layer)
    if cache_k is None:
        tps = ATTN_TILES_PER_STEP
        tq = ATTN_CHUNKS_PER_TILE * CHUNK
        assert seq % (tq * tps) == 0 and (BAND_ROWS // tq) % tps == 0
        n_steps = seq // (tq * tps)
        tile = pl.BlockSpec((1, tq * tps, WIDTH_A), lambda i, t: (i, t, 0))
        whole = pl.BlockSpec((1, seq, WIDTH_A), lambda i, t: (i, 0, 0))
        return pl.pallas_call(
            functools.partial(_attn_prompt_kernel, tq=tq, tps=tps, n_steps=n_steps),
            grid=(b, n_steps),
            in_specs=[tile, whole, whole, bias_spec],
            out_specs=tile,
            out_shape=out_shape,
            compiler_params=_params(2),
            name="band_attention",
        )(qa, ka, va, bias)
    assert seq == CHUNK
    bb = ATTN_CACHE_BATCH
    assert b % bb == 0
    new = pl.BlockSpec((bb, seq, WIDTH_A), lambda i: (i, 0, 0))
    cache = pl.BlockSpec((None, bb, WIDTH_A, BAND_ROWS), lambda i: (layer, i, 0, 0))
    return pl.pallas_call(
        functools.partial(_attn_cache_kernel, bb=bb),
        grid=(b // bb,),
        in_specs=[new, new, new, bias_spec, cache, cache],
        out_specs=new,
        out_shape=out_shape,
        compiler_params=_params(1),
        name="band_attention_cache",
    )(qa, ka, va, bias, cache_k, cache_v)


def _ff_chunks(d_ff):
    n_tiles = d_ff // MXU_DIM
    first = (n_tiles + 1) // 2 * MXU_DIM
    return [c for c in (first, d_ff - first) if c > 0] if n_tiles > 1 else [d_ff]


def _merge_rows(x, o_a, o_r, wo_ref, g_ref):
    mix = jnp.concatenate([o_a, o_r], axis=-1)
    x = x + jnp.dot(mix, wo_ref[...], preferred_element_type=F32)
    ms = jnp.mean(x * x, axis=-1, keepdims=True)
    return x, ((x * lax.rsqrt(ms + RMS_EPS)) * g_ref[...]).astype(BF16)


def _ffn_piece(h, y, wg_ref, wu_ref, wd_ref, lo, width):
    sl = slice(lo, lo + width)
    gate = jnp.dot(h, wg_ref[:, sl], preferred_element_type=F32)
    up = jnp.dot(h, wu_ref[:, sl], preferred_element_type=F32)
    act = ((gate * jax.nn.sigmoid(gate)) * up).astype(BF16)
    return y + jnp.dot(act, wd_ref[sl, :], preferred_element_type=F32)


FFN_ROW_STREAMS = 2


def _merge_ffn_kernel(x_ref, oa_ref, or_ref, wo_ref, g_ref, wg_ref, wu_ref, wd_ref, y_ref):
    rows = x_ref.shape[0] // FFN_ROW_STREAMS
    groups = [slice(s * rows, (s + 1) * rows) for s in range(FFN_ROW_STREAMS)]
    state = [_merge_rows(x_ref[g, :], oa_ref[g, :], or_ref[g, :], wo_ref, g_ref) for g in groups]
    lo = 0
    for width in _ff_chunks(wg_ref.shape[1]):
        state = [(_ffn_piece(h, y, wg_ref, wu_ref, wd_ref, lo, width), h) for y, h in state]
        lo += width
    for g, (y, _) in zip(groups, state):
        y_ref[g, :] = y


def _merge_ffn(x2d, o_a, o_r, prm, layer, tm=512):
    t_rows = x2d.shape[0]
    d_ff = prm["w_gate"].shape[-1]
    assert t_rows % tm == 0
    row = lambda i: (i, 0)
    return pl.pallas_call(
        _merge_ffn_kernel,
        grid=(t_rows // tm,),
        in_specs=[
            pl.BlockSpec((tm, D_MODEL), row),
            pl.BlockSpec((tm, WIDTH_A), row),
            pl.BlockSpec((tm, WIDTH_R), row),
            _layer_resident((D_MODEL, D_MODEL), layer),
            _layer_resident((1, D_MODEL), layer),
            _layer_resident((D_MODEL, d_ff), layer),
            _layer_resident((D_MODEL, d_ff), layer),
            _layer_resident((d_ff, D_MODEL), layer),
        ],
        out_specs=pl.BlockSpec((tm, D_MODEL), row),
        out_shape=jax.ShapeDtypeStruct((t_rows, D_MODEL), F32),
        compiler_params=_params(1),
        name="merge_ffn",
    )(x2d, o_a, o_r, prm["w_out"], prm["norm_ffn"], prm["w_gate"], prm["w_up"], prm["w_down"])


def _layer(x, prm, layer, bias, tail_rows, carried, cache_k=None, cache_v=None, state0=None):
    b, seq, _ = x.shape
    x2d = x.reshape(b * seq, D_MODEL)
    qa, ka, va, o_r, *carried = _proj(x2d, prm, layer, seq, tail_rows, carried, state0)
    r3 = lambda t: t.reshape(b, seq, SEG)
    o_a = _attention(r3(qa), r3(ka), r3(va), bias, layer, cache_k, cache_v)
    y = _merge_ffn(x2d, o_a.reshape(b * seq, WIDTH_A), o_r, prm, layer)
    return y.reshape(b, seq, D_MODEL), carried


def _tail_heads(t, b, tail_rows):
    depth = t.shape[0]
    if t.ndim == 4:
        return t.reshape(depth, b, HEADS_A, HEAD_DIM_A, tail_rows).transpose(0, 1, 4, 2, 3)
    return t.reshape(depth, b, tail_rows, HEADS_A, HEAD_DIM_A)


def kernel(x_prompt, x_sample, cache_a_k, cache_a_v, state_ret, norm_mix, w_in, q_norm, k_norm,
           rel_table, ret_norm, w_out, norm_ffn, w_gate, w_up, w_down):
    depth = w_in.shape[0]
    bp, sq_p, _ = x_prompt.shape
    bs, sq_s, _ = x_sample.shape
    assert cache_a_k.shape[2] == BAND_ROWS
    rows_p = min(BAND_ROWS, sq_p)

    vec = lambda t: t.reshape(depth, 1, -1)
    prm = dict(
        norm_mix=vec(norm_mix), norm_ffn=vec(norm_ffn), ret_norm=vec(ret_norm),
        q_norm=vec(jnp.tile(q_norm, (1, HEADS_A))), k_norm=vec(jnp.tile(k_norm, (1, HEADS_A))),
        w_in=w_in.astype(BF16), w_out=w_out.astype(BF16),
        w_gate=w_gate.astype(BF16), w_up=w_up.astype(BF16), w_down=w_down.astype(BF16))
    bias_p = _band_bias(rel_table, ATTN_CHUNKS_PER_TILE)
    bias_s = _band_bias(rel_table, 1)
    feature_major = lambda c: c.transpose(0, 1, 3, 4, 2).reshape(depth, bs, WIDTH_A, BAND_ROWS)
    cache_k, cache_v = feature_major(cache_a_k), feature_major(cache_a_v)

    xp, xs = x_prompt, x_sample
    out_p = out_s = None
    for l in range(depth):
        xp, out_p = _layer(xp, prm, l, bias_p, rows_p, out_p)
        xs, out_s = _layer(xs, prm, l, bias_s, sq_s, out_s, cache_k, cache_v, state_ret)
    return (xp, xs, _tail_heads(out_p[0], bp, rows_p), _tail_heads(out_p[1], bp, rows_p), out_p[2],
            _tail_heads(out_s[0], bs, sq_s), _tail_heads(out_s[1], bs, sq_s), out_s[2])
```

```python
import functools

import jax
import jax.numpy as jnp
from jax import lax
from jax.experimental import pallas as pl
from jax.experimental.pallas import tpu as pltpu

F32 = jnp.float32
BF16 = jnp.bfloat16

D_MODEL = 1024
CHUNK = 64
N_PREV_CHUNKS = 8
BAND_ROWS = N_PREV_CHUNKS * CHUNK
HEADS_A = 8
HEAD_DIM_A = 64
WIDTH_A = HEADS_A * HEAD_DIM_A
HEADS_R = 4
HEAD_DIM_R = 128
WIDTH_R = HEADS_R * HEAD_DIM_R
N_SEG = 7
SEG = 512
REL_CLIP = 2 * CHUNK
ROPE_BASE = 10000.0
RMS_EPS = 1e-6
GN_EPS = 1e-5
NEG_INF = -1e30
LOG2_E = 1.4426950408889634

VMEM_LIMIT_BYTES = 56 * 1024 * 1024
LANES = 128
MXU_DIM = 256


def _resident(shape):
    nd = len(shape)
    return pl.BlockSpec(shape, lambda *_: (0,) * nd, pipeline_mode=pl.Buffered(1))


def _layer_resident(shape, layer):
    nd = len(shape)
    return pl.BlockSpec((None,) + tuple(shape), lambda *_: (layer,) + (0,) * nd,
                        pipeline_mode=pl.Buffered(1))


def _params(n_axes):
    return pltpu.CompilerParams(dimension_semantics=("arbitrary",) * n_axes,
                                vmem_limit_bytes=VMEM_LIMIT_BYTES)


def _retention_block(q, k, v, g, s_prev, dec_ref, qd_ref, kd_ref, sd_ref, rn_ref):
    outs, s_new = [], []
    for hd in range(HEADS_R):
        sl = slice(hd * HEAD_DIM_R, (hd + 1) * HEAD_DIM_R)
        qh, kh, vh = q[:, sl], k[:, sl], v[:, sl]
        scores = lax.dot_general(qh, kh.astype(BF16), NT_DIMS, preferred_element_type=F32) * dec_ref[hd]
        intra = jnp.dot(scores.astype(BF16), vh, preferred_element_type=F32)
        cross = jnp.dot(qh, s_prev[hd].astype(BF16), preferred_element_type=F32) * qd_ref[:, sl]
        k_dec = (kh * kd_ref[:, sl]).astype(BF16)
        s_new.append(sd_ref[hd] * s_prev[hd] + lax.dot_general(
            k_dec, vh, (((0,), (0,)), ((), ())), preferred_element_type=F32))
        o = intra + cross
        mu = jnp.mean(o, axis=-1, keepdims=True)
        d = o - mu
        var = jnp.mean(d * d, axis=-1, keepdims=True)
        outs.append(d * lax.rsqrt(var + GN_EPS))
    o_norm = jnp.concatenate(outs, axis=-1) * rn_ref[...]
    return (g * jax.nn.sigmoid(g)) * o_norm, s_new


def _rebase_state(s, rc_ref, rs_ref):
    half = HEAD_DIM_R // 2
    s1, s2 = s[:half], s[half:]
    return jnp.concatenate([s1 * rc_ref[...] - s2 * rs_ref[...],
                            s1 * rs_ref[...] + s2 * rc_ref[...]], axis=0)


def _proj_kernel(*refs, slot, tail_period, tail_first, ret_rows, seq_tiles, has_state, n_cast):
    (x_ref, g_ref, w_ref, qn_ref, kn_ref, cos_ref, sin_ref,
     dec_ref, qd_ref, kd_ref, sd_ref, rn_ref, rc_ref, rs_ref) = refs[:14]
    n_in = 14
    if has_state:
        s0_ref = refs[n_in]
        n_in += 1
    if slot > 0:
        kt_prev, vt_prev, sout_prev = refs[n_in:n_in + 3]
        n_in += 3
    cast_in = refs[n_in:n_in + n_cast]
    n_in += n_cast
    qa_ref, ka_ref, va_ref, or_ref, kt_ref, vt_ref, sout_ref = refs[n_in:n_in + 7]
    cast_out = refs[n_in + 7:n_in + 7 + n_cast]
    if not has_state:
        tile_in_seq = pl.program_id(0) % seq_tiles

        @pl.when(tile_in_seq == 0)
        def _():
            refs[-1][...] = jnp.zeros(refs[-1].shape, F32)
    x = x_ref[...]
    ms = jnp.mean(x * x, axis=-1, keepdims=True)
    h = ((x * lax.rsqrt(ms + RMS_EPS)) * g_ref[...]).astype(BF16)

    def seg(j):
        return jnp.dot(h, w_ref[:, j * SEG:(j + 1) * SEG], preferred_element_type=F32)

    def head_rms(z, gain):
        z2 = z * z
        lane = lax.broadcasted_iota(jnp.int32, (z.shape[0], LANES), 1)
        low = lane < HEAD_DIM_A
        parts = []
        for c in range(SEG // LANES):
            v = z2[:, c * LANES:(c + 1) * LANES]
            s_lo = jnp.sum(jnp.where(low, v, 0.0), axis=-1, keepdims=True)
            s_hi = jnp.sum(jnp.where(low, 0.0, v), axis=-1, keepdims=True)
            parts.append(jnp.where(low, s_lo, s_hi))
        msq = jnp.concatenate(parts, axis=-1) * (1.0 / HEAD_DIM_A)
        return (z * lax.rsqrt(msq + RMS_EPS)) * gain

    def rotary(z):
        outs = []
        for hd in range(HEADS_R):
            zh = z[:, hd * HEAD_DIM_R:(hd + 1) * HEAD_DIM_R]
            outs.append(zh * cos_ref[...] + pltpu.roll(zh, HEAD_DIM_R // 2, axis=1) * sin_ref[...])
        return jnp.concatenate(outs, axis=-1)

    mixer_a = {}

    def project_qa():
        qa_ref[...] = (head_rms(seg(0), qn_ref[...]) * (HEAD_DIM_A ** -0.5 * LOG2_E)).astype(BF16)

    def project_ka():
        mixer_a["k"] = head_rms(seg(1), kn_ref[...])
        ka_ref[...] = mixer_a["k"].astype(BF16)

    def project_va():
        mixer_a["v"] = seg(2)
        va_ref[...] = mixer_a["v"].astype(BF16)

    pending = [project_qa, project_ka, project_va]

    qr = rotary(seg(3)).astype(BF16)
    kr = rotary(seg(4)) * (HEAD_DIM_R ** -0.5)
    vr = seg(5).astype(BF16)
    gr = seg(6)
    tables = (dec_ref, qd_ref, kd_ref, sd_ref, rn_ref)
    n_blocks = x.shape[0] // ret_rows
    if not has_state:
        state = refs[-1]
        s_cur = [state[hd] for hd in range(HEADS_R)]
    for j in range(n_blocks):
        rows = slice(j * ret_rows, (j + 1) * ret_rows)
        if has_state:
            s_cur = [s0_ref[j, hd] for hd in range(HEADS_R)]
        o, s_cur = _retention_block(qr[rows], kr[rows], vr[rows], gr[rows], s_cur, *tables)
        or_ref[rows, :] = o.astype(BF16)
        if has_state:
            for hd in range(HEADS_R):
                sout_ref[slot, j, hd] = _rebase_state(s_cur[hd], rc_ref, rs_ref)
        if pending:
            pending.pop(0)()
    for task in pending:
        task()
    for src, dst in zip(cast_in, cast_out):
        dst[...] = src[...].astype(BF16)
    if has_state:
        if slot > 0:
            sout_ref[0:slot] = sout_prev[...]
    else:
        for hd in range(HEADS_R):
            state[hd] = s_cur[hd]

        @pl.when(tile_in_seq == seq_tiles - 1)
        def _():
            if slot > 0:
                sout_ref[0:slot] = sout_prev[...]
            for hd in range(HEADS_R):
                sout_ref[slot, 0, hd] = _rebase_state(state[hd], rc_ref, rs_ref)

    ka, va = mixer_a["k"], mixer_a["v"]

    def write_tails():
        if slot > 0:
            kt_ref[0:slot] = kt_prev[...]
            vt_ref[0:slot] = vt_prev[...]
        if len(kt_ref.shape) == 4:
            kt_ref[slot, 0] = ka.T
            vt_ref[slot, 0] = va.T
            return
        for hd in range(HEADS_A):
            rows = pl.ds(hd, ka.shape[0], stride=HEADS_A)
            cols = slice(hd * HEAD_DIM_A, (hd + 1) * HEAD_DIM_A)
            kt_ref[slot, rows, :] = ka[:, cols]
            vt_ref[slot, rows, :] = va[:, cols]

    if tail_period == 1:
        write_tails()
    else:
        pl.when(pl.program_id(0) % tail_period >= tail_first)(write_tails)


RETENTION_BLOCK = 256


def _retention_tables(blk, seq):
    log_gamma = jnp.log(1.0 - 2.0 ** (-5.0 - jnp.arange(HEADS_R, dtype=F32)))
    idx = jnp.arange(blk, dtype=F32)
    diff = idx[:, None] - idx[None, :]
    decay = jnp.where(diff[None] >= 0,
                      jnp.exp(jnp.maximum(diff, 0.0)[None] * log_gamma[:, None, None]), 0.0)
    lanes = lambda t: jnp.repeat(t, HEAD_DIM_R, axis=-1)
    q_decay = lanes(jnp.exp((idx + 1.0)[:, None] * log_gamma[None, :]))
    k_decay = lanes(jnp.exp((blk - 1.0 - idx)[:, None] * log_gamma[None, :]))
    s_decay = jnp.broadcast_to(jnp.exp(blk * log_gamma)[:, None, None], (HEADS_R, 1, HEAD_DIM_R))
    half = HEAD_DIM_R // 2
    inv = ROPE_BASE ** (-jnp.arange(half, dtype=F32) / half)
    ang = jnp.asarray(-seq, dtype=F32) * inv
    reb_cos = jnp.broadcast_to(jnp.cos(ang)[:, None], (half, HEAD_DIM_R))
    reb_sin = jnp.broadcast_to(jnp.sin(ang)[:, None], (half, HEAD_DIM_R))
    return decay, q_decay, k_decay, s_decay, reb_cos, reb_sin


def _proj(x2d, prm, layer, seq, tail_rows, prev, state0=None, casts=()):
    t_rows = x2d.shape[0]
    tm = 512 if seq >= 512 else 256
    assert t_rows % tm == 0
    n_tiles = t_rows // tm
    n_seq = t_rows // seq
    has_state = state0 is not None
    if seq >= tm:
        assert seq % tm == 0 and tail_rows % tm == 0 and not has_state
        tail_period = seq // tm
        tail_first = (seq - tail_rows) // tm
        table_rows = seq
        ret_rows = min(RETENTION_BLOCK, tm)
    else:
        assert tm % seq == 0 and tail_rows == seq and has_state
        tail_period, tail_first = 1, 0
        table_rows = tm
        ret_rows = seq
    assert tm % ret_rows == 0
    n_tab = table_rows // tm
    seqs_per_tile = max(1, tm // seq)
    decay, q_decay, k_decay, s_decay, reb_cos, reb_sin = _retention_tables(ret_rows, seq)

    half = HEAD_DIM_R // 2
    inv = ROPE_BASE ** (-jnp.arange(half, dtype=F32) / half)
    pos = (jnp.arange(table_rows) % seq).astype(F32)
    ang = pos[:, None] * inv
    cos, sin = jnp.cos(ang), jnp.sin(ang)
    cos_t = jnp.concatenate([cos, cos], axis=-1)
    sin_t = jnp.concatenate([-sin, sin], axis=-1)

    row = lambda i: (i, 0)
    out_bf = jax.ShapeDtypeStruct((t_rows, SEG), BF16)
    slot = 0 if prev is None else prev[0].shape[0]
    state_dims = (seqs_per_tile, HEADS_R, HEAD_DIM_R, HEAD_DIM_R)
    state_full = (n_seq, HEADS_R, HEAD_DIM_R, HEAD_DIM_R)
    state_map = lambda i: (0, i // tail_period, 0, 0, 0)
    if seq >= tm:
        tail_full, tail_dims = (n_seq, SEG, tail_rows), (1, SEG, tm)
        tail_map = lambda i: (0, i // tail_period, 0, jnp.maximum(i % tail_period - tail_first, 0))
    else:
        tail_full, tail_dims = (t_rows * HEADS_A, HEAD_DIM_A), (tm * HEADS_A, HEAD_DIM_A)
        tail_map = lambda i: (0, i, 0)
    stacked = lambda n, dims, index_map: pl.BlockSpec((n,) + dims, index_map)
    tail_blk = stacked(slot + 1, tail_dims, tail_map)
    state_blk = stacked(slot + 1, state_dims, state_map)
    tail_shape = jax.ShapeDtypeStruct((slot + 1,) + tail_full, F32)
    state_shape = jax.ShapeDtypeStruct((slot + 1,) + state_full, F32)
    blk = pl.BlockSpec((tm, SEG), row)
    in_specs = [
        pl.BlockSpec((tm, D_MODEL), row),
        _layer_resident((1, D_MODEL), layer),
        _layer_resident((D_MODEL, N_SEG * SEG), layer),
        _layer_resident((1, SEG), layer),
        _layer_resident((1, SEG), layer),
        pl.BlockSpec((tm, HEAD_DIM_R), lambda i: (i % n_tab, 0)),
        pl.BlockSpec((tm, HEAD_DIM_R), lambda i: (i % n_tab, 0)),
        _resident(decay.shape), _resident(q_decay.shape), _resident(k_decay.shape),
        _resident(s_decay.shape), _layer_resident((1, WIDTH_R), layer),
        _resident(reb_cos.shape), _resident(reb_sin.shape),
    ]
    args = [x2d, prm["norm_mix"], prm["w_in"], prm["q_norm"], prm["k_norm"], cos_t, sin_t,
            decay, q_decay, k_decay, s_decay, prm["ret_norm"], reb_cos, reb_sin]
    scratch = []
    if has_state:
        in_specs.append(pl.BlockSpec((None,) + state_dims, lambda i: (layer, i, 0, 0, 0)))
        args.append(state0)
    else:
        scratch.append(pltpu.VMEM((HEADS_R, HEAD_DIM_R, HEAD_DIM_R), F32))
    if slot > 0:
        in_specs += [stacked(slot, tail_dims, tail_map), stacked(slot, tail_dims, tail_map),
                     stacked(slot, state_dims, state_map)]
        args += list(prev)
    cast_specs, cast_shapes = [], []
    for w in casts:
        flat = w.reshape(-1, w.shape[-1])
        rows, cols = flat.shape
        span = next(s for s in (1, 2, 4, 8) if rows * s % n_tiles == 0 and (rows * s // n_tiles) % 16 == 0)
        spec = pl.BlockSpec((rows * span // n_tiles, cols), lambda i, span=span: (i // span, 0))
        in_specs.append(spec)
        args.append(flat)
        cast_specs.append(spec)
        cast_shapes.append(jax.ShapeDtypeStruct(flat.shape, BF16))
    outs = pl.pallas_call(
        functools.partial(_proj_kernel, slot=slot, tail_period=tail_period, tail_first=tail_first,
                          ret_rows=ret_rows, seq_tiles=tail_period, has_state=has_state,
                          n_cast=len(casts)),
        grid=(n_tiles,),
        in_specs=in_specs,
        out_specs=[blk, blk, blk, blk, tail_blk, tail_blk, state_blk] + cast_specs,
        out_shape=[out_bf, out_bf, out_bf, out_bf, tail_shape, tail_shape, state_shape] + cast_shapes,
        scratch_shapes=scratch,
        compiler_params=_params(1),
        name="proj",
    )(*args)
    return outs[:7], [o.reshape(w.shape) for o, w in zip(outs[7:], casts)]


NT_DIMS = (((1,), (1,)), ((), ()))
ATTN_SCORES_AHEAD = 2


def _pair_scores(tile, pair, bias_ref):
    q_pair = tile["q"](pair)
    tq = q_pair.shape[0]
    lane = lax.broadcasted_iota(jnp.int32, (tq, LANES), 1)
    zero = jnp.zeros_like(q_pair)
    qz = jnp.concatenate([jnp.where(lane < HEAD_DIM_A, q_pair, zero),
                          jnp.where(lane >= HEAD_DIM_A, q_pair, zero)], axis=0)
    return tile["qk"](pair, qz) + bias_ref[pair, :, tile["lo"]:]


def _pair_output(tile, pair, s):
    tq = s.shape[0] // 2
    m = jnp.max(s, axis=-1, keepdims=True)
    p = jnp.exp2(s - m)
    denom = jnp.sum(p, axis=-1, keepdims=True)
    r = tile["pv"](pair, p.astype(BF16)) / denom
    lane = lax.broadcasted_iota(jnp.int32, (tq, LANES), 1)
    return jnp.where(lane < HEAD_DIM_A, r[:tq], r[tq:])


def _attention_units(tiles, bias_ref):
    n_pairs = HEADS_A // 2
    units = [(tile, pair) for tile in tiles for pair in range(n_pairs)]
    ahead = [_pair_scores(*unit, bias_ref) for unit in units[:ATTN_SCORES_AHEAD]]
    yield
    outs = []
    for idx, (tile, pair) in enumerate(units):
        s_cur = ahead.pop(0)
        if idx + ATTN_SCORES_AHEAD < len(units):
            ahead.append(_pair_scores(*units[idx + ATTN_SCORES_AHEAD], bias_ref))
        outs.append(_pair_output(tile, pair, s_cur))
        if pair == n_pairs - 1:
            tile["store"](jnp.concatenate(outs, axis=-1).astype(BF16))
            outs = []
        yield


def _run_attention_tiles(tiles, bias_ref):
    for _ in _attention_units(tiles, bias_ref):
        pass


def _attn_prompt_kernel(q_ref, k_ref, v_ref, bias_ref, o_ref, *, tq, tps, n_steps):
    nk = bias_ref.shape[-1]
    n_special = BAND_ROWS // tq
    step = pl.program_id(1)
    lanes = lambda pair: pl.ds(pair * LANES, LANES)

    def make_tile(i, row0, nkv):
        rows = pl.ds(row0, nkv)
        qrows = pl.ds(i * tq, tq)

        def store(o):
            o_ref[0, qrows, :] = o

        def qk(pair, qz):
            return lax.dot_general(qz, k_ref[0, rows, lanes(pair)], NT_DIMS, preferred_element_type=F32)

        def pv(pair, p):
            return jnp.dot(p, v_ref[0, rows, lanes(pair)], preferred_element_type=F32)
        return dict(q=lambda pair: q_ref[0, qrows, lanes(pair)], qk=qk, pv=pv, lo=nk - nkv, store=store)

    for j in range(min(n_special // tps, n_steps)):
        @pl.when(step == j)
        def _(j=j):
            _run_attention_tiles(
                [make_tile(i, 0, (j * tps + i + 1) * tq) for i in range(tps)], bias_ref)

    if n_steps > n_special // tps:
        @pl.when(step >= n_special // tps)
        def _():
            tiles = []
            for i in range(tps):
                row0 = pl.multiple_of((step * tps + i) * tq - BAND_ROWS, tq)
                tiles.append(make_tile(i, row0, nk))
            _run_attention_tiles(tiles, bias_ref)


def _attn_cache_kernel(q_ref, k_ref, v_ref, bias_ref, ck_ref, cv_ref, o_ref, *, bb):
    lanes = lambda pair: pl.ds(pair * LANES, LANES)

    def make_tile(bi):
        past = lambda ref, pair: ref[bi, lanes(pair), :].astype(BF16)

        def store(o):
            o_ref[bi] = o

        def qk(pair, qz):
            s_past = jnp.dot(qz, past(ck_ref, pair), preferred_element_type=F32)
            s_new = lax.dot_general(qz, k_ref[bi, :, lanes(pair)], NT_DIMS, preferred_element_type=F32)
            return jnp.concatenate([s_past, s_new], axis=-1)

        def pv(pair, p):
            return (lax.dot_general(p[:, :BAND_ROWS], past(cv_ref, pair), NT_DIMS, preferred_element_type=F32)
                    + jnp.dot(p[:, BAND_ROWS:], v_ref[bi, :, lanes(pair)], preferred_element_type=F32))
        return dict(q=lambda pair: q_ref[bi, :, lanes(pair)], qk=qk, pv=pv, lo=0, store=store)

    _run_attention_tiles([make_tile(bi) for bi in range(bb)], bias_ref)


def _band_bias(rel_table, chunks_per_tile):
    n_band = (N_PREV_CHUNKS + 1) * CHUNK
    table = rel_table.astype(F32)
    lead = table.shape[:-1]
    u_min = BAND_ROWS + REL_CLIP - (n_band - 1)
    n_far = CHUNK - 1 + BAND_ROWS - REL_CLIP
    g = jnp.concatenate([table[..., u_min:], jnp.broadcast_to(table[..., -1:], lead + (n_far,))], axis=-1)
    r = jnp.concatenate([g[..., ::-1], jnp.zeros(lead + (1,), F32)], axis=-1)
    period = r.shape[-1]
    tiled = jnp.tile(r, (1,) * len(lead) + (CHUNK + 1,))[..., :CHUNK * (period + 1)]
    hankel = tiled.reshape(lead + (CHUNK, period + 1))[..., :n_band]
    base = hankel[..., ::-1, :]
    pad_lead = ((0, 0),) * (len(lead) + 1)
    rows = [jnp.pad(base, pad_lead + ((i * CHUNK, (chunks_per_tile - 1 - i) * CHUNK),),
                    constant_values=NEG_INF) for i in range(chunks_per_tile)]
    bias = jnp.concatenate(rows, axis=-2) * LOG2_E
    return bias.reshape(lead[:-1] + (HEADS_A // 2, 2 * chunks_per_tile * CHUNK, bias.shape[-1]))


ATTN_CHUNKS_PER_TILE = 2
ATTN_TILES_PER_STEP = 4
ATTN_CACHE_BATCH = 4


def _attention(qa, ka, va, bias, layer, cache_k=None, cache_v=None):
    b, seq, _ = qa.shape
    out_shape = jax.ShapeDtypeStruct((b, seq, WIDTH_A), BF16)
    bias_spec = _layer_resident(bias.shape[1:], layer)
    if cache_k is None:
        tps = ATTN_TILES_PER_STEP
        tq = ATTN_CHUNKS_PER_TILE * CHUNK
        assert seq % (tq * tps) == 0 and (BAND_ROWS // tq) % tps == 0
        n_steps = seq // (tq * tps)
        tile = pl.BlockSpec((1, tq * tps, WIDTH_A), lambda i, t: (i, t, 0))
        whole = pl.BlockSpec((1, seq, WIDTH_A), lambda i, t: (i, 0, 0))
        return pl.pallas_call(
            functools.partial(_attn_prompt_kernel, tq=tq, tps=tps, n_steps=n_steps),
            grid=(b, n_steps),
            in_specs=[tile, whole, whole, bias_spec],
            out_specs=tile,
            out_shape=out_shape,
            compiler_params=_params(2),
            name="band_attention",
        )(qa, ka, va, bias)
    assert seq == CHUNK
    bb = ATTN_CACHE_BATCH
    assert b % bb == 0
    new = pl.BlockSpec((bb, seq, WIDTH_A), lambda i: (i, 0, 0))
    cache = pl.BlockSpec((None, bb, WIDTH_A, BAND_ROWS), lambda i: (layer, i, 0, 0))
    return pl.pallas_call(
        functools.partial(_attn_cache_kernel, bb=bb),
        grid=(b // bb,),
        in_specs=[new, new, new, bias_spec, cache, cache],
        out_specs=new,
        out_shape=out_shape,
        compiler_params=_params(1),
        name="band_attention_cache",
    )(qa, ka, va, bias, cache_k, cache_v)


def _ff_chunks(d_ff):
    n_tiles = d_ff // MXU_DIM
    first = (n_tiles + 1) // 2 * MXU_DIM
    return [c for c in (first, d_ff - first) if c > 0] if n_tiles > 1 else [d_ff]


def _merge_rows(x, o_a, o_r, wo_ref, g_ref):
    mix = jnp.concatenate([o_a, o_r], axis=-1)
    x = x + jnp.dot(mix, wo_ref[...], preferred_element_type=F32)
    ms = jnp.mean(x * x, axis=-1, keepdims=True)
    return x, ((x * lax.rsqrt(ms + RMS_EPS)) * g_ref[...]).astype(BF16)


def _ffn_piece(h, y, wg_ref, wu_ref, wd_ref, lo, width):
    sl = slice(lo, lo + width)
    gate = jnp.dot(h, wg_ref[:, sl], preferred_element_type=F32)
    up = jnp.dot(h, wu_ref[:, sl], preferred_element_type=F32)
    act = ((gate * jax.nn.sigmoid(gate)) * up).astype(BF16)
    return y + jnp.dot(act, wd_ref[sl, :], preferred_element_type=F32)


FFN_ROW_STREAMS = 2


def _merge_ffn_kernel(x_ref, oa_ref, or_ref, wo_ref, g_ref, wg_ref, wu_ref, wd_ref, y_ref):
    rows = x_ref.shape[0] // FFN_ROW_STREAMS
    groups = [slice(s * rows, (s + 1) * rows) for s in range(FFN_ROW_STREAMS)]
    state = [_merge_rows(x_ref[g, :], oa_ref[g, :], or_ref[g, :], wo_ref, g_ref) for g in groups]
    lo = 0
    for width in _ff_chunks(wg_ref.shape[1]):
        state = [(_ffn_piece(h, y, wg_ref, wu_ref, wd_ref, lo, width), h) for y, h in state]
        lo += width
    for g, (y, _) in zip(groups, state):
        y_ref[g, :] = y


def _merge_ffn(x2d, o_a, o_r, prm, layer, tm=512):
    t_rows = x2d.shape[0]
    d_ff = prm["w_gate"].shape[-1]
    assert t_rows % tm == 0
    row = lambda i: (i, 0)
    return pl.pallas_call(
        _merge_ffn_kernel,
        grid=(t_rows // tm,),
        in_specs=[
            pl.BlockSpec((tm, D_MODEL), row),
            pl.BlockSpec((tm, WIDTH_A), row),
            pl.BlockSpec((tm, WIDTH_R), row),
            _layer_resident((D_MODEL, D_MODEL), layer),
            _layer_resident((1, D_MODEL), layer),
            _layer_resident((D_MODEL, d_ff), layer),
            _layer_resident((D_MODEL, d_ff), layer),
            _layer_resident((d_ff, D_MODEL), layer),
        ],
        out_specs=pl.BlockSpec((tm, D_MODEL), row),
        out_shape=jax.ShapeDtypeStruct((t_rows, D_MODEL), F32),
        compiler_params=_params(1),
        name="merge_ffn",
    )(x2d, o_a, o_r, prm["w_out"], prm["norm_ffn"], prm["w_gate"], prm["w_up"], prm["w_down"])


FFN_WEIGHTS = ("w_out", "w_gate", "w_up", "w_down")


def _layer(x, prm, layer, bias, tail_rows, carried, cache_k=None, cache_v=None, state0=None):
    b, seq, _ = x.shape
    x2d = x.reshape(b * seq, D_MODEL)
    to_cast = [name for name in FFN_WEIGHTS if prm[name].dtype != BF16]
    (qa, ka, va, o_r, *carried), cast = _proj(x2d, prm, layer, seq, tail_rows, carried, state0,
                                              [prm[name] for name in to_cast])
    prm.update(zip(to_cast, cast))
    r3 = lambda t: t.reshape(b, seq, SEG)
    o_a = _attention(r3(qa), r3(ka), r3(va), bias, layer, cache_k, cache_v)
    y = _merge_ffn(x2d, o_a.reshape(b * seq, WIDTH_A), o_r, prm, layer)
    return y.reshape(b, seq, D_MODEL), carried


def _tail_heads(t, b, tail_rows):
    depth = t.shape[0]
    if t.ndim == 4:
        return t.reshape(depth, b, HEADS_A, HEAD_DIM_A, tail_rows).transpose(0, 1, 4, 2, 3)
    return t.reshape(depth, b, tail_rows, HEADS_A, HEAD_DIM_A)


def kernel(x_prompt, x_sample, cache_a_k, cache_a_v, state_ret, norm_mix, w_in, q_norm, k_norm,
           rel_table, ret_norm, w_out, norm_ffn, w_gate, w_up, w_down):
    depth = w_in.shape[0]
    bp, sq_p, _ = x_prompt.shape
    bs, sq_s, _ = x_sample.shape
    assert cache_a_k.shape[2] == BAND_ROWS
    rows_p = min(BAND_ROWS, sq_p)

    vec = lambda t: t.reshape(depth, 1, -1)
    prm = dict(
        norm_mix=vec(norm_mix), norm_ffn=vec(norm_ffn), ret_norm=vec(ret_norm),
        q_norm=vec(jnp.tile(q_norm, (1, HEADS_A))), k_norm=vec(jnp.tile(k_norm, (1, HEADS_A))),
        w_in=w_in.astype(BF16), w_out=w_out, w_gate=w_gate, w_up=w_up, w_down=w_down)
    bias_p = _band_bias(rel_table, ATTN_CHUNKS_PER_TILE)
    bias_s = _band_bias(rel_table, 1)
    feature_major = lambda c: c.transpose(0, 1, 3, 4, 2).reshape(depth, bs, WIDTH_A, BAND_ROWS)
    cache_k, cache_v = feature_major(cache_a_k), feature_major(cache_a_v)

    xp, xs = x_prompt, x_sample
    out_p = out_s = None
    for l in range(depth):
        xp, out_p = _layer(xp, prm, l, bias_p, rows_p, out_p)
        xs, out_s = _layer(xs, prm, l, bias_s, sq_s, out_s, cache_k, cache_v, state_ret)
    return (xp, xs, _tail_heads(out_p[0], bp, rows_p), _tail_heads(out_p[1], bp, rows_p), out_p[2],
            _tail_heads(out_s[0], bs, sq_s), _tail_heads(out_s[1], bs, sq_s), out_s[2])
```
